```python
import jax, jax.numpy as jnp
from jax import lax
import numpy as np

D_MODEL = 1024
BATCH = 2
SEQ = 16384
DEPTH = 1
DEC_BATCH = 128
DEC_SEQ = 8
PAST_LEN = 8192
PAGE_SIZE = 128

HEAD_DIM = 64
LRU_WIDTH = 512
LRU_BLOCKS = 8
LRU_BLOCK = LRU_WIDTH // LRU_BLOCKS
CONV_WIDTH = 4
LRU_C = 8.0
ATTN_HEADS = 8
ATTN_WIDTH = ATTN_HEADS * HEAD_DIM
MIX_WIDTH = LRU_WIDTH + ATTN_WIDTH
IN_WIDTH = 2 * LRU_WIDTH + 3 * ATTN_WIDTH
DILATED_GROUPS = ((128, 1), (512, 4), (2048, 16))
MAX_WINDOW = 2048
Q_BLOCK = 128
PLE_DIM = 256
PEER_HEADS = 8
PEER_KEYS = 128
PEER_EXPERTS = PEER_KEYS * PEER_KEYS
PEER_QDIM = 256
PEER_HALF = PEER_QDIM // 2
PEER_TOPK = 16
PEER_TOK_BLOCK = 256
EPS = 1e-6

kernel_name = "hymba_rglru_dilated_swa_peer_step"


def rmsnorm(x, g):
    xf = x.astype(jnp.float32)
    y = xf * lax.rsqrt(jnp.mean(xf * xf, axis=-1, keepdims=True) + EPS)
    return (y * g.astype(jnp.float32)).astype(x.dtype)


def causal_conv(xb, buf, w, b):
    T = xb.shape[1]
    xp = jnp.concatenate([buf.astype(xb.dtype), xb], axis=1)
    y = b
    for k in range(CONV_WIDTH):
        y = y + xp[:, k:k + T] * w[k]
    return y, xp[:, -(CONV_WIDTH - 1):]


def rg_lru(xc, h0, w_a, b_a, w_i, b_i, lam):
    B, T, C = xc.shape
    xf = xc.astype(jnp.float32)
    xblk = xf.reshape(B, T, LRU_BLOCKS, LRU_BLOCK)
    r = jax.nn.sigmoid(jnp.einsum('btnc,ncd->btnd', xblk, w_a.astype(jnp.float32)).reshape(B, T, C) + b_a)
    i = jax.nn.sigmoid(jnp.einsum('btnc,ncd->btnd', xblk, w_i.astype(jnp.float32)).reshape(B, T, C) + b_i)
    log_a = -LRU_C * r * jax.nn.softplus(-lam.astype(jnp.float32))
    a = jnp.exp(log_a)
    bterm = jnp.sqrt(-jnp.expm1(2.0 * log_a)) * (i * xf)
    bterm = bterm.at[:, 0].add(a[:, 0] * h0)

    def combine(left, right):
        a_l, b_l = left
        a_r, b_r = right
        return a_l * a_r, a_r * b_l + b_r

    _, h = lax.associative_scan(combine, (a, bterm), axis=1)
    return h, h[:, -1]


def dilated_attention(q, kbuf, vbuf, qidx):
    qf = q.astype(jnp.float32) * (HEAD_DIM ** -0.5)
    outs, lses = [], []
    for window, dil in DILATED_GROUPS:
        taps = jnp.arange(window // dil + 1, dtype=jnp.int32) * dil
        kidx = qidx[:, None] - taps[None, :]
        valid = kidx >= 0
        kidx = jnp.maximum(kidx, 0)
        kg = jnp.take(kbuf, kidx, axis=1).astype(jnp.float32)
        vg = jnp.take(vbuf, kidx, axis=1).astype(jnp.float32)
        s = jnp.einsum('bqhd,bqnhd->bqhn', qf, kg)
        s = jnp.where(valid[None, :, None, :], s, -jnp.inf)
        m = jnp.max(s, axis=-1, keepdims=True)
        pexp = jnp.exp(s - m)
        den = jnp.sum(pexp, axis=-1, keepdims=True)
        outs.append(jnp.einsum('bqhn,bqnhd->bqhd', pexp / den, vg))
        lses.append((m + jnp.log(den))[..., 0])
    w = jax.nn.softmax(jnp.stack(lses, axis=0), axis=0)
    return jnp.einsum('gbqh,gbqhd->bqhd', w, jnp.stack(outs, axis=0))


def mixer(xn, conv_buf, lru_h0, past_k, past_v, w_in, conv_w, conv_b, w_ra, b_ra, w_ri, b_ri,
          lam, g_out_lru, g_out_attn, w_out):
    B, T, _ = xn.shape
    proj = xn @ w_in
    xb, gate, q, k, v = jnp.split(
        proj, [LRU_WIDTH, 2 * LRU_WIDTH, 2 * LRU_WIDTH + ATTN_WIDTH, 2 * LRU_WIDTH + 2 * ATTN_WIDTH], axis=-1)
    xc, new_conv = causal_conv(xb, conv_buf, conv_w, conv_b)
    h, h_last = rg_lru(xc, lru_h0, w_ra, b_ra, w_ri, b_ri, lam)
    y_lru = (h * jax.nn.gelu(gate.astype(jnp.float32))).astype(xn.dtype)
    q = q.reshape(B, T, ATTN_HEADS, HEAD_DIM)
    k = k.reshape(B, T, ATTN_HEADS, HEAD_DIM)
    v = v.reshape(B, T, ATTN_HEADS, HEAD_DIM)
    if past_k is None:
        nblk = T // Q_BLOCK
        qb = q.reshape(B, nblk, Q_BLOCK, ATTN_HEADS, HEAD_DIM).transpose(1, 0, 2, 3, 4)
        starts = jnp.arange(nblk, dtype=jnp.int32) * Q_BLOCK
        ob = lax.map(lambda a: dilated_attention(a[0], k, v, a[1] + jnp.arange(Q_BLOCK, dtype=jnp.int32)),
                     (qb, starts))
        o = ob.transpose(1, 0, 2, 3, 4).reshape(B, T, ATTN_WIDTH)
        keep = min(MAX_WINDOW, T)
        new_k, new_v = k[:, T - keep:], v[:, T - keep:]
    else:
        L = past_k.shape[1]
        kbuf = jnp.concatenate([past_k, k.astype(past_k.dtype)], axis=1)
        vbuf = jnp.concatenate([past_v, v.astype(past_v.dtype)], axis=1)
        o = dilated_attention(q, kbuf, vbuf, L + jnp.arange(T, dtype=jnp.int32)).reshape(B, T, ATTN_WIDTH)
        new_k, new_v = k, v
    o = o.astype(xn.dtype)
    y = jnp.concatenate([rmsnorm(y_lru, g_out_lru), rmsnorm(o, g_out_attn)], axis=-1) @ w_out
    return y, new_conv, h_last.astype(xn.dtype), new_k, new_v


def peer(xn, w_q, sub_keys, u_emb, v_emb):
    N = xn.shape[0]
    npad = (-N) % PEER_TOK_BLOCK
    blocks = jnp.pad(xn, ((0, npad), (0, 0))).reshape(-1, PEER_TOK_BLOCK, D_MODEL)

    def one(xb):
        T = xb.shape[0]
        q = (xb @ w_q).reshape(T, PEER_HEADS, 2, PEER_HALF).astype(jnp.float32)
        s = jnp.einsum('thcd,hckd->thck', q, sub_keys.astype(jnp.float32))
        top_s, top_i = lax.top_k(s, PEER_TOPK)
        cand_s = (top_s[:, :, 0, :, None] + top_s[:, :, 1, None, :]).reshape(T, PEER_HEADS, PEER_TOPK * PEER_TOPK)
        cand_i = (top_i[:, :, 0, :, None] * PEER_KEYS + top_i[:, :, 1, None, :]).reshape(T, PEER_HEADS, PEER_TOPK * PEER_TOPK)
        best_s, pos = lax.top_k(cand_s, PEER_TOPK)
        idx = jnp.take_along_axis(cand_i, pos, axis=-1)
        g = jax.nn.softmax(best_s, axis=-1)
        u = jnp.take(u_emb, idx, axis=0)
        act = jax.nn.gelu(jnp.einsum('td,thkd->thk', xb, u).astype(jnp.float32))
        v = jnp.take(v_emb, idx, axis=0)
        return jnp.einsum('thk,thkd->td', (g * act).astype(xb.dtype), v)

    return lax.map(one, blocks).reshape(-1, D_MODEL)[:N]


def trunk(x, p, conv_state, lru_state, win_k, win_v, params):
    (g_mix, w_in, conv_w, conv_b, w_ra, b_ra, w_ri, b_ri, lru_lambda, g_out_lru, g_out_attn, w_out,
     g_ffn, w_peer_q, peer_sub_keys, peer_u, peer_v, g_ple, w_ple_gate, w_ple_proj, g_final) = params
    B, T, _ = x.shape
    h = x
    n_conv, n_lru, n_k, n_v = [], [], [], []
    for l in range(DEPTH):
        if conv_state is None:
            cb = jnp.zeros((B, CONV_WIDTH - 1, LRU_WIDTH), x.dtype)
            h0 = jnp.zeros((B, LRU_WIDTH), jnp.float32)
            pk, pv = None, None
        else:
            cb = conv_state[l]
            h0 = lru_state[l].astype(jnp.float32)
            pk, pv = win_k[l], win_v[l]
        y, nc, nh, nk, nv = mixer(rmsnorm(h, g_mix[l]), cb, h0, pk, pv, w_in[l], conv_w[l], conv_b[l],
                                  w_ra[l], b_ra[l], w_ri[l], b_ri[l], lru_lambda[l],
                                  g_out_lru[l], g_out_attn[l], w_out[l])
        h = h + y
        f = peer(rmsnorm(h, g_ffn[l]).reshape(B * T, D_MODEL), w_peer_q[l], peer_sub_keys[l],
                 peer_u[l], peer_v[l]).reshape(B, T, D_MODEL)
        h = h + f
        gate = jax.nn.sigmoid((rmsnorm(h, g_ple[l]) @ w_ple_gate[l]).astype(jnp.float32)).astype(h.dtype)
        h = h + gate * (p[l] @ w_ple_proj[l])
        n_conv.append(nc); n_lru.append(nh); n_k.append(nk); n_v.append(nv)
    return (rmsnorm(h, g_final), jnp.stack(n_conv, 0), jnp.stack(n_lru, 0),
            jnp.stack(n_k, 0), jnp.stack(n_v, 0))


def setup_inputs(seed: int = 0) -> dict:
    key = jax.random.key(seed)
    ks = jax.random.split(key, 32)
    f32 = jnp.float32
    win_buf = min(MAX_WINDOW, PAST_LEN)

    def nrm(k, shape, s):
        return jax.random.normal(k, shape, f32) * s

    def gain(k, shape):
        return 1.0 + 0.01 * jax.random.normal(k, shape, f32)

    u = jax.random.uniform(ks[0], (DEPTH, LRU_WIDTH), f32, 0.9, 0.999)
    a0 = u ** (1.0 / LRU_C)
    lam = jnp.log(a0) - jnp.log1p(-a0)
    return {
        'x_prompt': nrm(ks[1], (BATCH, SEQ, D_MODEL), 1.0),
        'x_sample': nrm(ks[2], (DEC_BATCH, DEC_SEQ, D_MODEL), 1.0),
        'state_conv': nrm(ks[3], (DEPTH, DEC_BATCH, CONV_WIDTH - 1, LRU_WIDTH), 1.0),
        'state_lru': nrm(ks[4], (DEPTH, DEC_BATCH, LRU_WIDTH), 0.5),
        'cache_win_k': nrm(ks[5], (DEPTH, DEC_BATCH, win_buf, ATTN_HEADS, HEAD_DIM), 1.0),
        'cache_win_v': nrm(ks[6], (DEPTH, DEC_BATCH, win_buf, ATTN_HEADS, HEAD_DIM), 1.0),
        'p_prompt': nrm(ks[7], (DEPTH, BATCH, SEQ, PLE_DIM), 1.0),
        'p_sample': nrm(ks[8], (DEPTH, DEC_BATCH, DEC_SEQ, PLE_DIM), 1.0),
        'g_mix': gain(ks[9], (DEPTH, D_MODEL)),
        'w_in': nrm(ks[10], (DEPTH, D_MODEL, IN_WIDTH), D_MODEL ** -0.5),
        'conv_w': nrm(ks[11], (DEPTH, CONV_WIDTH, LRU_WIDTH), CONV_WIDTH ** -0.5),
        'conv_b': nrm(ks[12], (DEPTH, LRU_WIDTH), 0.01),
        'w_ra': nrm(ks[13], (DEPTH, LRU_BLOCKS, LRU_BLOCK, LRU_BLOCK), LRU_BLOCK ** -0.5),
        'b_ra': nrm(ks[14], (DEPTH, LRU_WIDTH), 0.01),
        'w_ri': nrm(ks[15], (DEPTH, LRU_BLOCKS, LRU_BLOCK, LRU_BLOCK), LRU_BLOCK ** -0.5),
        'b_ri': nrm(ks[16], (DEPTH, LRU_WIDTH), 0.01),
        'lru_lambda': lam,
        'g_out_lru': gain(ks[17], (DEPTH, LRU_WIDTH)),
        'g_out_attn': gain(ks[18], (DEPTH, ATTN_WIDTH)),
        'w_out': nrm(ks[19], (DEPTH, MIX_WIDTH, D_MODEL), MIX_WIDTH ** -0.5),
        'g_ffn': gain(ks[20], (DEPTH, D_MODEL)),
        'w_peer_q': nrm(ks[21], (DEPTH, D_MODEL, PEER_HEADS * PEER_QDIM), D_MODEL ** -0.5),
        'peer_sub_keys': nrm(ks[22], (DEPTH, PEER_HEADS, 2, PEER_KEYS, PEER_HALF), PEER_HALF ** -0.5),
        'peer_u': nrm(ks[23], (DEPTH, PEER_EXPERTS, D_MODEL), D_MODEL ** -0.5),
        'peer_v': nrm(ks[24], (DEPTH, PEER_EXPERTS, D_MODEL), 0.5),
        'g_ple': gain(ks[25], (DEPTH, D_MODEL)),
        'w_ple_gate': nrm(ks[26], (DEPTH, D_MODEL, D_MODEL), D_MODEL ** -0.5),
        'w_ple_proj': nrm(ks[27], (DEPTH, PLE_DIM, D_MODEL), PLE_DIM ** -0.5),
        'g_final': gain(ks[28], (D_MODEL,)),
    }


def reference(x_prompt, x_sample, state_conv, state_lru, cache_win_k, cache_win_v, p_prompt, p_sample,
              g_mix, w_in, conv_w, conv_b, w_ra, b_ra, w_ri, b_ri, lru_lambda, g_out_lru, g_out_attn,
              w_out, g_ffn, w_peer_q, peer_sub_keys, peer_u, peer_v, g_ple, w_ple_gate, w_ple_proj, g_final):
    params = (g_mix, w_in, conv_w, conv_b, w_ra, b_ra, w_ri, b_ri, lru_lambda, g_out_lru, g_out_attn, w_out,
              g_ffn, w_peer_q, peer_sub_keys, peer_u, peer_v, g_ple, w_ple_gate, w_ple_proj, g_final)
    y_prompt, prompt_conv, prompt_lru, prompt_win_k, prompt_win_v = trunk(
        x_prompt, p_prompt, None, None, None, None, params)
    y_sample, sample_conv, sample_lru, sample_win_k, sample_win_v = trunk(
        x_sample, p_sample, state_conv, state_lru, cache_win_k, cache_win_v, params)
    return (y_prompt, y_sample, prompt_conv, prompt_lru, prompt_win_k, prompt_win_v,
            sample_conv, sample_lru, sample_win_k, sample_win_v)
```

```python
import functools

import jax
import jax.numpy as jnp
from jax import lax
from jax.experimental import pallas as pl
from jax.experimental.pallas import tpu as pltpu

F32 = jnp.float32
BF16 = jnp.bfloat16
I32 = jnp.int32

EPS = 1e-6
D_MODEL = 1024
HEAD_DIM = 64
ATTN_HEADS = 8
WIDTH = 512
LRU_C = 8.0
DILATED_GROUPS = ((128, 1), (512, 4), (2048, 16))
MAX_WINDOW = 2048
PEER_HEADS = 8
PEER_KEYS = 128
PEER_HALF = 128
PEER_TOPK = 16
PEER_SEL = PEER_HEADS * PEER_TOPK
NEG = -1e30

LANES = 128
SUBLANES = 8
VMEM_LIMIT = 48 * 1024 * 1024

TOKEN_TILE = 512
LRU_TILE = 256
ATTN_TILE = 256
TOPK_TILE = 128
PEER_TILE = 16


def _params(sem):
    return pltpu.CompilerParams(dimension_semantics=sem, vmem_limit_bytes=VMEM_LIMIT)


def _rms(x, g):
    return x * lax.rsqrt(jnp.mean(x * x, axis=-1, keepdims=True) + EPS) * g


def _dot(a, b):
    return jnp.dot(a, b, preferred_element_type=F32)


def _dot_nt(a, b):
    return lax.dot_general(a, b, (((1,), (1,)), ((), ())), preferred_element_type=F32)


def _split(a):
    hi = a.astype(BF16)
    lo = (a - hi.astype(F32)).astype(BF16)
    return hi, lo


def _const_spec(shape):
    nd = len(shape)
    return pl.BlockSpec(shape, lambda *_: (0,) * nd)


def _inproj_kernel(x_ref, g_ref, w_ref, xb_ref, gate_ref, q_ref, k_ref, v_ref, kb_ref, vb_ref):
    xn = _rms(x_ref[...], g_ref[...]).astype(BF16)

    def proj(c):
        return _dot(xn, w_ref[:, c * WIDTH:(c + 1) * WIDTH])

    xb_ref[...] = proj(0)
    gate_ref[...] = proj(1)
    q_ref[...] = (proj(2) * (HEAD_DIM ** -0.5)).astype(BF16)
    k = proj(3)
    k_ref[...] = k
    kb_ref[...] = k.astype(BF16)
    v = proj(4)
    v_ref[...] = v
    vb_ref[...] = v.astype(BF16)


def _inproj(x, g, w_bf):
    n = x.shape[0]
    tm = min(TOKEN_TILE, n)
    row = lambda d: pl.BlockSpec((tm, d), lambda i: (i, 0))
    f = jax.ShapeDtypeStruct((n, WIDTH), F32)
    b = jax.ShapeDtypeStruct((n, WIDTH), BF16)
    return pl.pallas_call(
        _inproj_kernel,
        grid=(n // tm,),
        in_specs=[row(D_MODEL), _const_spec((1, D_MODEL)), _const_spec(w_bf.shape)],
        out_specs=[row(WIDTH)] * 7,
        out_shape=[f, f, b, f, f, b, b],
        compiler_params=_params(("parallel",)),
        name="inproj",
    )(x, g, w_bf)


def _lru_kernel(xb_ref, gate_ref, cs_ref, h0_ref, cw_ref, cb_ref, wah_ref, wal_ref, ba_ref,
                wih_ref, wil_ref, bi_ref, lam_ref, g_ref, yl_ref, hl_ref,
                xpad, hc, a_s, b_s, h_s, *, tb):
    j = pl.program_id(1)

    @pl.when(j == 0)
    def _():
        xpad[0:SUBLANES, :] = jnp.zeros((SUBLANES, WIDTH), F32)
        xpad[SUBLANES - 3:SUBLANES, :] = cs_ref[...]
        hc[...] = jnp.broadcast_to(h0_ref[...], (SUBLANES, WIDTH))

    x = xb_ref[...]
    xpad[SUBLANES:SUBLANES + tb, :] = x
    w = cw_ref[...]
    xc = cb_ref[...] + xpad[SUBLANES - 3:SUBLANES - 3 + tb, :] * w[0:1]
    xc = xc + xpad[SUBLANES - 2:SUBLANES - 2 + tb, :] * w[1:2]
    xc = xc + xpad[SUBLANES - 1:SUBLANES - 1 + tb, :] * w[2:3]
    xc = xc + x * w[3:4]
    xpad[0:SUBLANES, :] = xpad[tb:tb + SUBLANES, :]

    x_hi, x_lo = _split(xc)

    def gate_dot(wh_ref, wl_ref):
        wh = wh_ref[...]
        return _dot(x_hi, wh) + _dot(x_lo, wh) + _dot(x_hi, wl_ref[...])

    r = jax.nn.sigmoid(gate_dot(wah_ref, wal_ref) + ba_ref[...])
    i = jax.nn.sigmoid(gate_dot(wih_ref, wil_ref) + bi_ref[...])
    nl = -lam_ref[...]
    softplus = jnp.maximum(nl, 0.0) + jnp.log1p(jnp.exp(-jnp.abs(nl)))
    log_a = -LRU_C * r * softplus
    a = jnp.exp(log_a)
    a_s[...] = a
    b_s[...] = jnp.sqrt(jnp.tanh(-log_a) * (a * a + 1.0)) * (i * xc)

    rows = lax.broadcasted_iota(I32, (SUBLANES, WIDTH), 0)

    def group(gi, hb):
        off = pl.multiple_of(gi * SUBLANES, SUBLANES)
        av = a_s[pl.ds(off, SUBLANES), :]
        bv = b_s[pl.ds(off, SUBLANES), :]
        for d in (1, 2, 4):
            a_sh = pltpu.roll(av, d, axis=0)
            b_sh = pltpu.roll(bv, d, axis=0)
            m = rows >= d
            bv = jnp.where(m, av * b_sh + bv, bv)
            av = jnp.where(m, av * a_sh, av)
        h = av * hb + bv
        h_s[pl.ds(off, SUBLANES), :] = h
        return jnp.broadcast_to(h[SUBLANES - 1:SUBLANES, :], (SUBLANES, WIDTH))

    hb = lax.fori_loop(0, tb // SUBLANES, group, hc[...])
    hc[...] = hb

    y = h_s[...] * jax.nn.gelu(gate_ref[...])
    yl_ref[...] = _rms(y, g_ref[...]).astype(BF16)

    @pl.when(j == pl.num_programs(1) - 1)
    def _():
        hl_ref[...] = hb[0:1, :]


def _lru(xb, gate, conv_state, h0, conv_w, conv_b, w_ra, b_ra, w_ri, b_ri, lam, g_out):
    bsz, t, _ = xb.shape
    tb = min(LRU_TILE, t)
    wah, wal = _split(jax.scipy.linalg.block_diag(*w_ra))
    wih, wil = _split(jax.scipy.linalg.block_diag(*w_ri))
    seq = pl.BlockSpec((None, tb, WIDTH), lambda b, j: (b, j, 0))
    vec = _const_spec((1, WIDTH))
    sq = _const_spec((WIDTH, WIDTH))
    kern = functools.partial(_lru_kernel, tb=tb)
    return pl.pallas_call(
        kern,
        grid=(bsz, t // tb),
        in_specs=[seq, seq,
                  pl.BlockSpec((None, 3, WIDTH), lambda b, j: (b, 0, 0)),
                  pl.BlockSpec((None, 1, WIDTH), lambda b, j: (b, 0, 0)),
                  _const_spec((4, WIDTH)), vec, sq, sq, vec, sq, sq, vec, vec, vec],
        out_specs=[seq, pl.BlockSpec((None, 1, WIDTH), lambda b, j: (b, 0, 0))],
        out_shape=[jax.ShapeDtypeStruct((bsz, t, WIDTH), BF16),
                   jax.ShapeDtypeStruct((bsz, 1, WIDTH), F32)],
        scratch_shapes=[pltpu.VMEM((tb + SUBLANES, WIDTH), F32),
                        pltpu.VMEM((SUBLANES, WIDTH), F32),
                        pltpu.VMEM((tb, WIDTH), F32),
                        pltpu.VMEM((tb, WIDTH), F32),
                        pltpu.VMEM((tb, WIDTH), F32)],
        compiler_params=_params(("parallel", "arbitrary")),
        name="rg_lru",
    )(xb, gate, conv_state, h0.reshape(bsz, 1, WIDTH), conv_w, conv_b.reshape(1, WIDTH),
      wah, wal, b_ra.reshape(1, WIDTH), wih, wil, b_ri.reshape(1, WIDTH),
      lam.reshape(1, WIDTH), g_out.reshape(1, WIDTH))


def _tap_count(d):
    one = jnp.ones(d.shape, F32)
    zero = jnp.zeros(d.shape, F32)
    cnt = zero
    for window, dil in DILATED_GROUPS:
        hit = jnp.where(d <= window, one, zero)
        if dil > 1:
            hit = jnp.where((d & (dil - 1)) == 0, hit, zero)
        cnt = cnt + hit
    return jnp.where(d >= 0, cnt, zero)


def _pair_masks(rows):
    lane = lax.broadcasted_iota(I32, (rows, LANES), 1)
    return lane < HEAD_DIM


def _attn_kernel(q_ref, k_ref, v_ref, g_ref, o_ref, m_s, l_s, acc_s, *, tq, nkb):
    i = pl.program_id(1)
    j = pl.program_id(2)

    @pl.when(j == 0)
    def _():
        m_s[...] = jnp.full(m_s.shape, NEG, F32)
        l_s[...] = jnp.zeros(l_s.shape, F32)
        acc_s[...] = jnp.zeros(acc_s.shape, F32)

    kb = i - (nkb - 1) + j
    lo = _pair_masks(tq)

    @pl.when(kb >= 0)
    def _():
        r = lax.broadcasted_iota(I32, (tq, tq), 0)
        c = lax.broadcasted_iota(I32, (tq, tq), 1)
        cnt = _tap_count((i - kb) * tq + r - c)
        valid = cnt > 0.0
        zb = jnp.zeros((tq, LANES), BF16)
        for hp in range(ATTN_HEADS // 2):
            sl = slice(hp * LANES, (hp + 1) * LANES)
            qp = q_ref[:, sl]
            kp = k_ref[:, sl]
            vp = v_ref[:, sl]
            qs = (jnp.where(lo, qp, zb), jnp.where(lo, zb, qp))
            ps, alphas = [], []
            for hh in range(2):
                h = 2 * hp + hh
                s = jnp.where(valid, _dot_nt(qs[hh], kp), NEG)
                m_old = m_s[h][:, 0:1]
                m_new = jnp.maximum(m_old, jnp.max(s, axis=-1, keepdims=True))
                p = jnp.exp(s - m_new) * cnt
                alpha = jnp.exp(m_old - m_new)
                l_new = alpha * l_s[h][:, 0:1] + jnp.sum(p, axis=-1, keepdims=True)
                m_s[h] = jnp.broadcast_to(m_new, (tq, LANES))
                l_s[h] = jnp.broadcast_to(l_new, (tq, LANES))
                ps.append(p.astype(BF16))
                alphas.append(alpha)
            pcat = jnp.concatenate(ps, axis=1)
            vbd = jnp.concatenate([jnp.where(lo, vp, zb), jnp.where(lo, zb, vp)], axis=0)
            alpha_pair = jnp.where(lo, alphas[0], alphas[1])
            acc_s[:, sl] = alpha_pair * acc_s[:, sl] + _dot(pcat, vbd)

    @pl.when(j == nkb - 1)
    def _():
        parts = []
        for hp in range(ATTN_HEADS // 2):
            sl = slice(hp * LANES, (hp + 1) * LANES)
            l_pair = jnp.where(lo, l_s[2 * hp][:, 0:1], l_s[2 * hp + 1][:, 0:1])
            parts.append(acc_s[:, sl] / l_pair)
        o = jnp.concatenate(parts, axis=1)
        o_ref[...] = _rms(o, g_ref[...]).astype(BF16)


def _attn_prompt(q, k, v, g):
    bsz, t, _ = q.shape
    tq = min(ATTN_TILE, t)
    nkb = MAX_WINDOW // tq + 1
    qspec = pl.BlockSpec((None, tq, WIDTH), lambda b, i, j: (b, i, 0))
    kspec = pl.BlockSpec((None, tq, WIDTH), lambda b, i, j: (b, jnp.maximum(i - (nkb - 1) + j, 0), 0))
    kern = functools.partial(_attn_kernel, tq=tq, nkb=nkb)
    return pl.pallas_call(
        kern,
        grid=(bsz, t // tq, nkb),
        in_specs=[qspec, kspec, kspec, _const_spec((1, WIDTH))],
        out_specs=qspec,
        out_shape=jax.ShapeDtypeStruct((bsz, t, WIDTH), BF16),
        scratch_shapes=[pltpu.VMEM((ATTN_HEADS, tq, LANES), F32),
                        pltpu.VMEM((ATTN_HEADS, tq, LANES), F32),
                        pltpu.VMEM((tq, WIDTH), F32)],
        compiler_params=_params(("parallel", "parallel", "arbitrary")),
        name="attn_prompt",
    )(q, k, v, g.reshape(1, WIDTH))


def _attn_sample_kernel(q_ref, kc_ref, vc_ref, kn_ref, vn_ref, g_ref, o_ref, *, t, past):
    lo_q = _pair_masks(t)
    lo_c = _pair_masks(past)
    rq = lax.broadcasted_iota(I32, (t, past), 0)
    cp = lax.broadcasted_iota(I32, (t, past), 1)
    cnt_c = _tap_count(past + rq - cp)
    r2 = lax.broadcasted_iota(I32, (t, t), 0)
    c2 = lax.broadcasted_iota(I32, (t, t), 1)
    cnt_n = _tap_count(r2 - c2)
    val_c = cnt_c > 0.0
    val_n = cnt_n > 0.0
    zq = jnp.zeros((t, LANES), BF16)
    zc = jnp.zeros((past, LANES), BF16)
    parts = []
    for hp in range(ATTN_HEADS // 2):
        sl = slice(hp * LANES, (hp + 1) * LANES)
        qp = q_ref[:, sl]
        kc = kc_ref[:, sl].astype(BF16)
        vc = vc_ref[:, sl].astype(BF16)
        kn = kn_ref[:, sl]
        vn = vn_ref[:, sl]
        qs = (jnp.where(lo_q, qp, zq), jnp.where(lo_q, zq, qp))
        pcs, pns, ls = [], [], []
        for hh in range(2):
            s_c = jnp.where(val_c, _dot_nt(qs[hh], kc), NEG)
            s_n = jnp.where(val_n, _dot_nt(qs[hh], kn), NEG)
            m = jnp.maximum(jnp.max(s_c, axis=-1, keepdims=True), jnp.max(s_n, axis=-1, keepdims=True))
            p_c = jnp.exp(s_c - m) * cnt_c
            p_n = jnp.exp(s_n - m) * cnt_n
            ls.append(jnp.sum(p_c, axis=-1, keepdims=True) + jnp.sum(p_n, axis=-1, keepdims=True))
            pcs.append(p_c.astype(BF16))
            pns.append(p_n.astype(BF16))
        vbd_c = jnp.concatenate([jnp.where(lo_c, vc, zc), jnp.where(lo_c, zc, vc)], axis=0)
        vbd_n = jnp.concatenate([jnp.where(lo_q, vn, zq), jnp.where(lo_q, zq, vn)], axis=0)
        pv = _dot(jnp.concatenate(pcs, axis=1), vbd_c) + _dot(jnp.concatenate(pns, axis=1), vbd_n)
        parts.append(pv / jnp.where(lo_q, ls[0], ls[1]))
    o = jnp.concatenate(parts, axis=1)
    o_ref[...] = _rms(o, g_ref[...]).astype(BF16)


def _attn_sample(q, k_cache, v_cache, k_new, v_new, g):
    bsz, t, _ = q.shape
    past = k_cache.shape[1]
    new = pl.BlockSpec((None, t, WIDTH), lambda b: (b, 0, 0))
    old = pl.BlockSpec((None, past, WIDTH), lambda b: (b, 0, 0))
    kern = functools.partial(_attn_sample_kernel, t=t, past=past)
    return pl.pallas_call(
        kern,
        grid=(bsz,),
        in_specs=[new, old, old, new, new, _const_spec((1, WIDTH))],
        out_specs=new,
        out_shape=jax.ShapeDtypeStruct((bsz, t, WIDTH), BF16),
        compiler_params=_params(("parallel",)),
        name="attn_sample",
    )(q, k_cache, v_cache, k_new, v_new, g.reshape(1, WIDTH))


def _outproj_kernel(yl_ref, oa_ref, x_ref, wt_ref, wb_ref, g_ref, wq_ref, h_ref, xn_ref, qp_ref):
    h = x_ref[...] + (_dot(yl_ref[...], wt_ref[...]) + _dot(oa_ref[...], wb_ref[...]))
    h_ref[...] = h
    xn = _rms(h, g_ref[...])
    xn_ref[...] = xn
    qp_ref[...] = _dot(xn.astype(BF16), wq_ref[...])


def _outproj(yl, oa, x, w_out_bf, g_ffn, wq_bf):
    n = x.shape[0]
    tm = min(TOKEN_TILE, n)
    nq = wq_bf.shape[1]
    row = lambda d: pl.BlockSpec((tm, d), lambda i: (i, 0))
    return pl.pallas_call(
        _outproj_kernel,
        grid=(n // tm,),
        in_specs=[row(WIDTH), row(WIDTH), row(D_MODEL),
                  _const_spec((WIDTH, D_MODEL)), _const_spec((WIDTH, D_MODEL)),
                  _const_spec((1, D_MODEL)), _const_spec(wq_bf.shape)],
        out_specs=[row(D_MODEL), row(D_MODEL), row(nq)],
        out_shape=[jax.ShapeDtypeStruct((n, D_MODEL), F32),
                   jax.ShapeDtypeStruct((n, D_MODEL), F32),
                   jax.ShapeDtypeStruct((n, nq), F32)],
        compiler_params=_params(("parallel",)),
        name="outproj",
    )(yl, oa, x, w_out_bf[:WIDTH], w_out_bf[WIDTH:], g_ffn.reshape(1, D_MODEL), wq_bf)


def _topk_kernel(qp_ref, kh_ref, kl_ref, idx_ref, g_ref, top_s, top_i, cand_s, cand_i, best_s,
                 out_g, out_i, *, tn):
    iota_k = lax.broadcasted_iota(I32, (PEER_KEYS, tn), 0)
    ncand = PEER_TOPK * PEER_TOPK
    iota_c = lax.broadcasted_iota(I32, (ncand, tn), 0)
    for h in range(PEER_HEADS):
        for c in range(2):
            hc = 2 * h + c
            q_hi, q_lo = _split(qp_ref[:, hc * PEER_HALF:(hc + 1) * PEER_HALF])
            kh = kh_ref[hc]
            s = _dot_nt(kh, q_hi) + _dot_nt(kh, q_lo) + _dot_nt(kl_ref[hc], q_hi)
            for r in range(PEER_TOPK):
                m = jnp.max(s, axis=0, keepdims=True)
                am = jnp.min(jnp.where(s == m, iota_k, PEER_KEYS), axis=0, keepdims=True)
                s = jnp.where(iota_k == am, -jnp.inf, s)
                top_s[c, r:r + 1, :] = m
                top_i[c, r:r + 1, :] = am
        s1 = top_s[1]
        i1 = top_i[1]
        for a in range(PEER_TOPK):
            rows = slice(a * PEER_TOPK, (a + 1) * PEER_TOPK)
            cand_s[rows, :] = top_s[0, a:a + 1, :] + s1
            cand_i[rows, :] = top_i[0, a:a + 1, :] * PEER_KEYS + i1
        cs = cand_s[...]
        ci = cand_i[...]
        for r in range(PEER_TOPK):
            m = jnp.max(cs, axis=0, keepdims=True)
            pos = jnp.min(jnp.where(cs == m, iota_c, ncand), axis=0, keepdims=True)
            sel = iota_c == pos
            out_i[h * PEER_TOPK + r:h * PEER_TOPK + r + 1, :] = jnp.max(
                jnp.where(sel, ci, -1), axis=0, keepdims=True)
            best_s[r:r + 1, :] = m
            cs = jnp.where(sel, -jnp.inf, cs)
        bs = best_s[...]
        e = jnp.exp(bs - bs[0:1, :])
        out_g[h * PEER_TOPK:(h + 1) * PEER_TOPK, :] = e / jnp.sum(e, axis=0, keepdims=True)
    g_ref[...] = out_g[...].T
    idx_ref[...] = out_i[...].T


def _topk(qp, sub_keys):
    n = qp.shape[0]
    tn = min(TOPK_TILE, n)
    kh, kl = _split(sub_keys.reshape(PEER_HEADS * 2, PEER_KEYS, PEER_HALF))
    kern = functools.partial(_topk_kernel, tn=tn)
    ncand = PEER_TOPK * PEER_TOPK
    return pl.pallas_call(
        kern,
        grid=(n // tn,),
        in_specs=[pl.BlockSpec((tn, qp.shape[1]), lambda i: (i, 0)),
                  _const_spec(kh.shape), _const_spec(kl.shape)],
        out_specs=[pl.BlockSpec((tn, PEER_SEL), lambda i: (i, 0))] * 2,
        out_shape=[jax.ShapeDtypeStruct((n, PEER_SEL), I32),
                   jax.ShapeDtypeStruct((n, PEER_SEL), F32)],
        scratch_shapes=[pltpu.VMEM((2, PEER_TOPK, tn), F32),
                        pltpu.VMEM((2, PEER_TOPK, tn), I32),
                        pltpu.VMEM((ncand, tn), F32),
                        pltpu.VMEM((ncand, tn), I32),
                        pltpu.VMEM((PEER_TOPK, tn), F32),
                        pltpu.VMEM((PEER_SEL, tn), F32),
                        pltpu.VMEM((PEER_SEL, tn), I32)],
        compiler_params=_params(("parallel",)),
        name="peer_topk",
    )(qp, kh, kl)


def _peer_kernel(idx_ref, nxt_ref, xn_ref, g_ref, tab_ref, f_ref, buf, sem, *, tb):
    s = pl.program_id(0)
    n = pl.num_programs(0)
    slot = lax.rem(s, 2)

    def row_copy(e, slot_, row):
        return pltpu.make_async_copy(tab_ref.at[pl.ds(e, 1)], buf.at[slot_, pl.ds(row, 1)],
                                     sem.at[slot_])

    def issue(ids, slot_):
        def tok(t, carry):
            for r in range(PEER_SEL):
                row_copy(ids[t, r], slot_, t * PEER_SEL + r).start()
            return carry
        lax.fori_loop(0, tb, tok, 0)

    @pl.when(s == 0)
    def _():
        issue(idx_ref, 0)

    @pl.when(s + 1 < n)
    def _():
        issue(nxt_ref, 1 - slot)

    def drain(t, carry):
        for r in range(PEER_SEL):
            row_copy(0, slot, t * PEER_SEL + r).wait()
        return carry
    lax.fori_loop(0, tb, drain, 0)

    g_cols = g_ref[...].T
    for t in range(tb):
        rows = slice(t * PEER_SEL, (t + 1) * PEER_SEL)
        x = xn_ref[t:t + 1, :]
        u = buf[slot, rows, 0:D_MODEL]
        act = jax.nn.gelu(jnp.sum(u * x, axis=-1, keepdims=True))
        coef = g_cols[:, t:t + 1] * act
        v = buf[slot, rows, D_MODEL:2 * D_MODEL]
        f_ref[t:t + 1, :] = jnp.sum(coef * v, axis=0, keepdims=True)


def _peer(idx, g, xn, table):
    n = xn.shape[0]
    tb = PEER_TILE
    steps = n // tb
    kern = functools.partial(_peer_kernel, tb=tb)
    smem = lambda fn: pl.BlockSpec((tb, PEER_SEL), fn, memory_space=pltpu.SMEM)
    return pl.pallas_call(
        kern,
        grid=(steps,),
        in_specs=[smem(lambda s: (s, 0)),
                  smem(lambda s: (jnp.minimum(s + 1, steps - 1), 0)),
                  pl.BlockSpec((tb, D_MODEL), lambda s: (s, 0)),
                  pl.BlockSpec((tb, PEER_SEL), lambda s: (s, 0)),
                  pl.BlockSpec(memory_space=pl.ANY)],
        out_specs=pl.BlockSpec((tb, D_MODEL), lambda s: (s, 0)),
        out_shape=jax.ShapeDtypeStruct((n, D_MODEL), F32),
        scratch_shapes=[pltpu.VMEM((2, tb * PEER_SEL, 2 * D_MODEL), F32),
                        pltpu.SemaphoreType.DMA((2,))],
        compiler_params=_params(("arbitrary",)),
        name="peer_gather",
    )(idx, idx, xn, g, table)


def _final_kernel(h_ref, f_ref, p_ref, gp_ref, wg_ref, wp_ref, gf_ref, y_ref):
    h = h_ref[...] + f_ref[...]
    gate = jax.nn.sigmoid(_dot(_rms(h, gp_ref[...]).astype(BF16), wg_ref[...]))
    h = h + gate * _dot(p_ref[...].astype(BF16), wp_ref[...])
    y_ref[...] = _rms(h, gf_ref[...])


def _final(h, f, p, g_ple, wg_bf, wp_bf, g_final):
    n = h.shape[0]
    tm = min(TOKEN_TILE, n)
    row = lambda d: pl.BlockSpec((tm, d), lambda i: (i, 0))
    return pl.pallas_call(
        _final_kernel,
        grid=(n // tm,),
        in_specs=[row(D_MODEL), row(D_MODEL), row(p.shape[1]), _const_spec((1, D_MODEL)),
                  _const_spec(wg_bf.shape), _const_spec(wp_bf.shape), _const_spec((1, D_MODEL))],
        out_specs=row(D_MODEL),
        out_shape=jax.ShapeDtypeStruct((n, D_MODEL), F32),
        compiler_params=_params(("parallel",)),
        name="ple_final",
    )(h, f, p, g_ple.reshape(1, D_MODEL), wg_bf, wp_bf, g_final.reshape(1, D_MODEL))


def _trunk(x, p, conv_state, lru_state, win_k, win_v, w):
    bsz, t, _ = x.shape
    n = bsz * t
    xb, gate, q, k, v, kb, vb = _inproj(x.reshape(n, D_MODEL), w["g_mix"], w["w_in"])
    seq = lambda a: a.reshape(bsz, t, WIDTH)
    xb3 = seq(xb)
    if conv_state is None:
        conv_state = jnp.zeros((bsz, 3, WIDTH), F32)
        lru_state = jnp.zeros((bsz, WIDTH), F32)
    yl, h_last = _lru(xb3, seq(gate), conv_state, lru_state, w["conv_w"], w["conv_b"],
                      w["w_ra"], w["b_ra"], w["w_ri"], w["b_ri"], w["lam"], w["g_out_lru"])
    if win_k is None:
        oa = _attn_prompt(seq(q), seq(kb), seq(vb), w["g_out_attn"])
        keep = min(MAX_WINDOW, t)
        new_k, new_v = seq(k)[:, t - keep:], seq(v)[:, t - keep:]
    else:
        past = win_k.shape[1]
        oa = _attn_sample(seq(q), win_k.reshape(bsz, past, WIDTH), win_v.reshape(bsz, past, WIDTH),
                          seq(kb), seq(vb), w["g_out_attn"])
        new_k, new_v = seq(k), seq(v)
    h1, xn2, qp = _outproj(yl.reshape(n, WIDTH), oa.reshape(n, WIDTH), x.reshape(n, D_MODEL),
                           w["w_out"], w["g_ffn"], w["w_peer_q"])
    idx, g = _topk(qp, w["sub_keys"])
    f = _peer(idx, g, xn2, w["table"])
    y = _final(h1, f, p.reshape(n, p.shape[-1]), w["g_ple"], w["w_ple_gate"], w["w_ple_proj"],
               w["g_final"])
    heads = lambda a: a.reshape(1, bsz, a.shape[1], ATTN_HEADS, HEAD_DIM)
    return (y.reshape(bsz, t, D_MODEL), xb3[:, t - 3:][None], h_last.reshape(1, bsz, WIDTH),
            heads(new_k), heads(new_v))


def kernel(x_prompt, x_sample, state_conv, state_lru, cache_win_k, cache_win_v, p_prompt, p_sample,
           g_mix, w_in, conv_w, conv_b, w_ra, b_ra, w_ri, b_ri, lru_lambda, g_out_lru, g_out_attn,
           w_out, g_ffn, w_peer_q, peer_sub_keys, peer_u, peer_v, g_ple, w_ple_gate, w_ple_proj,
           g_final):
    w = {
        "g_mix": g_mix[0].reshape(1, D_MODEL), "w_in": w_in[0].astype(BF16),
        "conv_w": conv_w[0], "conv_b": conv_b[0], "w_ra": w_ra[0], "b_ra": b_ra[0],
        "w_ri": w_ri[0], "b_ri": b_ri[0], "lam": lru_lambda[0],
        "g_out_lru": g_out_lru[0], "g_out_attn": g_out_attn[0],
        "w_out": w_out[0].astype(BF16), "g_ffn": g_ffn[0], "w_peer_q": w_peer_q[0].astype(BF16),
        "sub_keys": peer_sub_keys[0],
        "table": jnp.concatenate([peer_u[0], peer_v[0]], axis=1),
        "g_ple": g_ple[0], "w_ple_gate": w_ple_gate[0].astype(BF16),
        "w_ple_proj": w_ple_proj[0].astype(BF16), "g_final": g_final,
    }
    yp, pc, plru, pk, pv = _trunk(x_prompt, p_prompt[0], None, None, None, None, w)
    ys, sc, slru, sk, sv = _trunk(x_sample, p_sample[0], state_conv[0], state_lru[0],
                                  cache_win_k[0], cache_win_v[0], w)
    return (yp, ys, pc, plru, pk, pv, sc, slru, sk, sv)
```

```python
import functools

import jax
import jax.numpy as jnp
from jax import lax
from jax.experimental import pallas as pl
from jax.experimental.pallas import tpu as pltpu

F32 = jnp.float32
BF16 = jnp.bfloat16
I32 = jnp.int32

EPS = 1e-6
D_MODEL = 1024
HEAD_DIM = 64
ATTN_HEADS = 8
WIDTH = 512
LRU_C = 8.0
DILATED_GROUPS = ((128, 1), (512, 4), (2048, 16))
MAX_WINDOW = 2048
PEER_HEADS = 8
PEER_KEYS = 128
PEER_HALF = 128
PEER_TOPK = 16
PEER_SEL = PEER_HEADS * PEER_TOPK
NEG = -1e30

LANES = 128
SUBLANES = 8
VMEM_LIMIT = 48 * 1024 * 1024

TOKEN_TILE = 512
LRU_TILE = 256
ATTN_TILE = 256
TOPK_TILE = 128
PEER_TILE = 16


def _params(sem):
    return pltpu.CompilerParams(dimension_semantics=sem, vmem_limit_bytes=VMEM_LIMIT)


def _rms(x, g):
    return x * lax.rsqrt(jnp.mean(x * x, axis=-1, keepdims=True) + EPS) * g


def _dot(a, b):
    return jnp.dot(a, b, preferred_element_type=F32)


def _dot_nt(a, b):
    return lax.dot_general(a, b, (((1,), (1,)), ((), ())), preferred_element_type=F32)


def _split(a):
    hi = a.astype(BF16)
    lo = (a - hi.astype(F32)).astype(BF16)
    return hi, lo


def _const_spec(shape):
    nd = len(shape)
    return pl.BlockSpec(shape, lambda *_: (0,) * nd)


def _inproj_kernel(x_ref, g_ref, w_ref, xb_ref, gate_ref, q_ref, k_ref, v_ref, kb_ref, vb_ref):
    xn = _rms(x_ref[...], g_ref[...]).astype(BF16)

    def proj(c):
        return _dot(xn, w_ref[:, c * WIDTH:(c + 1) * WIDTH])

    xb_ref[...] = proj(0)
    gate_ref[...] = proj(1)
    q_ref[...] = (proj(2) * (HEAD_DIM ** -0.5)).astype(BF16)
    k = proj(3)
    k_ref[...] = k
    kb_ref[...] = k.astype(BF16)
    v = proj(4)
    v_ref[...] = v
    vb_ref[...] = v.astype(BF16)


def _inproj(x, g, w_bf):
    n = x.shape[0]
    tm = min(TOKEN_TILE, n)
    row = lambda d: pl.BlockSpec((tm, d), lambda i: (i, 0))
    f = jax.ShapeDtypeStruct((n, WIDTH), F32)
    b = jax.ShapeDtypeStruct((n, WIDTH), BF16)
    return pl.pallas_call(
        _inproj_kernel,
        grid=(n // tm,),
        in_specs=[row(D_MODEL), _const_spec((1, D_MODEL)), _const_spec(w_bf.shape)],
        out_specs=[row(WIDTH)] * 7,
        out_shape=[f, f, b, f, f, b, b],
        compiler_params=_params(("parallel",)),
        name="inproj",
    )(x, g, w_bf)


def _lru_kernel(xb_ref, gate_ref, cs_ref, h0_ref, cw_ref, cb_ref, wah_ref, wal_ref, ba_ref,
                wih_ref, wil_ref, bi_ref, lam_ref, g_ref, yl_ref, hl_ref,
                xpad, hc, a_s, b_s, h_s, *, tb):
    j = pl.program_id(1)

    @pl.when(j == 0)
    def _():
        xpad[0:SUBLANES, :] = jnp.zeros((SUBLANES, WIDTH), F32)
        xpad[SUBLANES - 3:SUBLANES, :] = cs_ref[...]
        hc[...] = jnp.broadcast_to(h0_ref[...], (SUBLANES, WIDTH))

    x = xb_ref[...]
    xpad[SUBLANES:SUBLANES + tb, :] = x
    w = cw_ref[...]
    xc = cb_ref[...] + xpad[SUBLANES - 3:SUBLANES - 3 + tb, :] * w[0:1]
    xc = xc + xpad[SUBLANES - 2:SUBLANES - 2 + tb, :] * w[1:2]
    xc = xc + xpad[SUBLANES - 1:SUBLANES - 1 + tb, :] * w[2:3]
    xc = xc + x * w[3:4]
    xpad[0:SUBLANES, :] = xpad[tb:tb + SUBLANES, :]

    x_hi, x_lo = _split(xc)

    def gate_dot(wh_ref, wl_ref):
        wh = wh_ref[...]
        return _dot(x_hi, wh) + _dot(x_lo, wh) + _dot(x_hi, wl_ref[...])

    r = jax.nn.sigmoid(gate_dot(wah_ref, wal_ref) + ba_ref[...])
    i = jax.nn.sigmoid(gate_dot(wih_ref, wil_ref) + bi_ref[...])
    nl = -lam_ref[...]
    softplus = jnp.maximum(nl, 0.0) + jnp.log1p(jnp.exp(-jnp.abs(nl)))
    log_a = -LRU_C * r * softplus
    a = jnp.exp(log_a)
    a_s[...] = a
    b_s[...] = jnp.sqrt(jnp.tanh(-log_a) * (a * a + 1.0)) * (i * xc)

    rows = lax.broadcasted_iota(I32, (SUBLANES, WIDTH), 0)

    def group(gi, hb):
        off = pl.multiple_of(gi * SUBLANES, SUBLANES)
        av = a_s[pl.ds(off, SUBLANES), :]
        bv = b_s[pl.ds(off, SUBLANES), :]
        for d in (1, 2, 4):
            a_sh = pltpu.roll(av, d, axis=0)
            b_sh = pltpu.roll(bv, d, axis=0)
            m = rows >= d
            bv = jnp.where(m, av * b_sh + bv, bv)
            av = jnp.where(m, av * a_sh, av)
        h = av * hb + bv
        h_s[pl.ds(off, SUBLANES), :] = h
        return jnp.broadcast_to(h[SUBLANES - 1:SUBLANES, :], (SUBLANES, WIDTH))

    hb = lax.fori_loop(0, tb // SUBLANES, group, hc[...])
    hc[...] = hb

    y = h_s[...] * jax.nn.gelu(gate_ref[...])
    yl_ref[...] = _rms(y, g_ref[...]).astype(BF16)

    @pl.when(j == pl.num_programs(1) - 1)
    def _():
        hl_ref[...] = hb[0:1, :]


def _lru(xb, gate, conv_state, h0, conv_w, conv_b, w_ra, b_ra, w_ri, b_ri, lam, g_out):
    bsz, t, _ = xb.shape
    tb = min(LRU_TILE, t)
    wah, wal = _split(jax.scipy.linalg.block_diag(*w_ra))
    wih, wil = _split(jax.scipy.linalg.block_diag(*w_ri))
    seq = pl.BlockSpec((None, tb, WIDTH), lambda b, j: (b, j, 0))
    vec = _const_spec((1, WIDTH))
    sq = _const_spec((WIDTH, WIDTH))
    kern = functools.partial(_lru_kernel, tb=tb)
    return pl.pallas_call(
        kern,
        grid=(bsz, t // tb),
        in_specs=[seq, seq,
                  pl.BlockSpec((None, 3, WIDTH), lambda b, j: (b, 0, 0)),
                  pl.BlockSpec((None, 1, WIDTH), lambda b, j: (b, 0, 0)),
                  _const_spec((4, WIDTH)), vec, sq, sq, vec, sq, sq, vec, vec, vec],
        out_specs=[seq, pl.BlockSpec((None, 1, WIDTH), lambda b, j: (b, 0, 0))],
        out_shape=[jax.ShapeDtypeStruct((bsz, t, WIDTH), BF16),
                   jax.ShapeDtypeStruct((bsz, 1, WIDTH), F32)],
        scratch_shapes=[pltpu.VMEM((tb + SUBLANES, WIDTH), F32),
                        pltpu.VMEM((SUBLANES, WIDTH), F32),
                        pltpu.VMEM((tb, WIDTH), F32),
                        pltpu.VMEM((tb, WIDTH), F32),
                        pltpu.VMEM((tb, WIDTH), F32)],
        compiler_params=_params(("parallel", "arbitrary")),
        name="rg_lru",
    )(xb, gate, conv_state, h0.reshape(bsz, 1, WIDTH), conv_w, conv_b.reshape(1, WIDTH),
      wah, wal, b_ra.reshape(1, WIDTH), wih, wil, b_ri.reshape(1, WIDTH),
      lam.reshape(1, WIDTH), g_out.reshape(1, WIDTH))


def _tap_count(d):
    one = jnp.ones(d.shape, F32)
    zero = jnp.zeros(d.shape, F32)
    cnt = zero
    for window, dil in DILATED_GROUPS:
        hit = jnp.where(d <= window, one, zero)
        if dil > 1:
            hit = jnp.where((d & (dil - 1)) == 0, hit, zero)
        cnt = cnt + hit
    return jnp.where(d >= 0, cnt, zero)


def _pair_masks(rows):
    lane = lax.broadcasted_iota(I32, (rows, LANES), 1)
    return lane < HEAD_DIM


def _attn_kernel(q_ref, k_ref, v_ref, g_ref, o_ref, m_s, l_s, acc_s, *, tq, nkb):
    i = pl.program_id(1)
    j = pl.program_id(2)

    @pl.when(j == 0)
    def _():
        m_s[...] = jnp.full(m_s.shape, NEG, F32)
        l_s[...] = jnp.zeros(l_s.shape, F32)
        acc_s[...] = jnp.zeros(acc_s.shape, F32)

    kb = i - (nkb - 1) + j
    lo = _pair_masks(tq)

    @pl.when(kb >= 0)
    def _():
        r = lax.broadcasted_iota(I32, (tq, tq), 0)
        c = lax.broadcasted_iota(I32, (tq, tq), 1)
        cnt = _tap_count((i - kb) * tq + r - c)
        valid = cnt > 0.0
        zb = jnp.zeros((tq, LANES), BF16)
        for hp in range(ATTN_HEADS // 2):
            sl = slice(hp * LANES, (hp + 1) * LANES)
            qp = q_ref[:, sl]
            kp = k_ref[:, sl]
            vp = v_ref[:, sl]
            qs = (jnp.where(lo, qp, zb), jnp.where(lo, zb, qp))
            ps, alphas = [], []
            for hh in range(2):
                h = 2 * hp + hh
                s = jnp.where(valid, _dot_nt(qs[hh], kp), NEG)
                m_old = m_s[h][:, 0:1]
                m_new = jnp.maximum(m_old, jnp.max(s, axis=-1, keepdims=True))
                p = jnp.exp(s - m_new) * cnt
                alpha = jnp.exp(m_old - m_new)
                l_new = alpha * l_s[h][:, 0:1] + jnp.sum(p, axis=-1, keepdims=True)
                m_s[h] = jnp.broadcast_to(m_new, (tq, LANES))
                l_s[h] = jnp.broadcast_to(l_new, (tq, LANES))
                ps.append(p.astype(BF16))
                alphas.append(alpha)
            pcat = jnp.concatenate(ps, axis=1)
            vbd = jnp.concatenate([jnp.where(lo, vp, zb), jnp.where(lo, zb, vp)], axis=0)
            alpha_pair = jnp.where(lo, alphas[0], alphas[1])
            acc_s[:, sl] = alpha_pair * acc_s[:, sl] + _dot(pcat, vbd)

    @pl.when(j == nkb - 1)
    def _():
        parts = []
        for hp in range(ATTN_HEADS // 2):
            sl = slice(hp * LANES, (hp + 1) * LANES)
            l_pair = jnp.where(lo, l_s[2 * hp][:, 0:1], l_s[2 * hp + 1][:, 0:1])
            parts.append(acc_s[:, sl] / l_pair)
        o = jnp.concatenate(parts, axis=1)
        o_ref[...] = _rms(o, g_ref[...]).astype(BF16)


def _attn_prompt(q, k, v, g):
    bsz, t, _ = q.shape
    tq = min(ATTN_TILE, t)
    nkb = MAX_WINDOW // tq + 1
    qspec = pl.BlockSpec((None, tq, WIDTH), lambda b, i, j: (b, i, 0))
    kspec = pl.BlockSpec((None, tq, WIDTH), lambda b, i, j: (b, jnp.maximum(i - (nkb - 1) + j, 0), 0))
    kern = functools.partial(_attn_kernel, tq=tq, nkb=nkb)
    return pl.pallas_call(
        kern,
        grid=(bsz, t // tq, nkb),
        in_specs=[qspec, kspec, kspec, _const_spec((1, WIDTH))],
        out_specs=qspec,
        out_shape=jax.ShapeDtypeStruct((bsz, t, WIDTH), BF16),
        scratch_shapes=[pltpu.VMEM((ATTN_HEADS, tq, LANES), F32),
                        pltpu.VMEM((ATTN_HEADS, tq, LANES), F32),
                        pltpu.VMEM((tq, WIDTH), F32)],
        compiler_params=_params(("parallel", "parallel", "arbitrary")),
        name="attn_prompt",
    )(q, k, v, g.reshape(1, WIDTH))


def _attn_sample_kernel(q_ref, kc_ref, vc_ref, kn_ref, vn_ref, g_ref, o_ref, *, t, past):
    lo_q = _pair_masks(t)
    lo_c = _pair_masks(past)
    rq = lax.broadcasted_iota(I32, (t, past), 0)
    cp = lax.broadcasted_iota(I32, (t, past), 1)
    cnt_c = _tap_count(past + rq - cp)
    r2 = lax.broadcasted_iota(I32, (t, t), 0)
    c2 = lax.broadcasted_iota(I32, (t, t), 1)
    cnt_n = _tap_count(r2 - c2)
    val_c = cnt_c > 0.0
    val_n = cnt_n > 0.0
    zq = jnp.zeros((t, LANES), BF16)
    zc = jnp.zeros((past, LANES), BF16)
    parts = []
    for hp in range(ATTN_HEADS // 2):
        sl = slice(hp * LANES, (hp + 1) * LANES)
        qp = q_ref[:, sl]
        kc = kc_ref[:, sl].astype(BF16)
        vc = vc_ref[:, sl].astype(BF16)
        kn = kn_ref[:, sl]
        vn = vn_ref[:, sl]
        qs = (jnp.where(lo_q, qp, zq), jnp.where(lo_q, zq, qp))
        pcs, pns, ls = [], [], []
        for hh in range(2):
            s_c = jnp.where(val_c, _dot_nt(qs[hh], kc), NEG)
            s_n = jnp.where(val_n, _dot_nt(qs[hh], kn), NEG)
            m = jnp.maximum(jnp.max(s_c, axis=-1, keepdims=True), jnp.max(s_n, axis=-1, keepdims=True))
            p_c = jnp.exp(s_c - m) * cnt_c
            p_n = jnp.exp(s_n - m) * cnt_n
            ls.append(jnp.sum(p_c, axis=-1, keepdims=True) + jnp.sum(p_n, axis=-1, keepdims=True))
            pcs.append(p_c.astype(BF16))
            pns.append(p_n.astype(BF16))
        vbd_c = jnp.concatenate([jnp.where(lo_c, vc, zc), jnp.where(lo_c, zc, vc)], axis=0)
        vbd_n = jnp.concatenate([jnp.where(lo_q, vn, zq), jnp.where(lo_q, zq, vn)], axis=0)
        pv = _dot(jnp.concatenate(pcs, axis=1), vbd_c) + _dot(jnp.concatenate(pns, axis=1), vbd_n)
        parts.append(pv / jnp.where(lo_q, ls[0], ls[1]))
    o = jnp.concatenate(parts, axis=1)
    o_ref[...] = _rms(o, g_ref[...]).astype(BF16)


def _attn_sample(q, k_cache, v_cache, k_new, v_new, g):
    bsz, t, _ = q.shape
    past = k_cache.shape[1]
    new = pl.BlockSpec((None, t, WIDTH), lambda b: (b, 0, 0))
    old = pl.BlockSpec((None, past, WIDTH), lambda b: (b, 0, 0))
    kern = functools.partial(_attn_sample_kernel, t=t, past=past)
    return pl.pallas_call(
        kern,
        grid=(bsz,),
        in_specs=[new, old, old, new, new, _const_spec((1, WIDTH))],
        out_specs=new,
        out_shape=jax.ShapeDtypeStruct((bsz, t, WIDTH), BF16),
        compiler_params=_params(("parallel",)),
        name="attn_sample",
    )(q, k_cache, v_cache, k_new, v_new, g.reshape(1, WIDTH))


def _outproj_kernel(yl_ref, oa_ref, x_ref, wt_ref, wb_ref, g_ref, wq_ref, h_ref, xn_ref, qp_ref):
    h = x_ref[...] + (_dot(yl_ref[...], wt_ref[...]) + _dot(oa_ref[...], wb_ref[...]))
    h_ref[...] = h
    xn = _rms(h, g_ref[...])
    xn_ref[...] = xn
    qp_ref[...] = _dot(xn.astype(BF16), wq_ref[...])


def _outproj(yl, oa, x, w_out_bf, g_ffn, wq_bf):
    n = x.shape[0]
    tm = min(TOKEN_TILE, n)
    nq = wq_bf.shape[1]
    row = lambda d: pl.BlockSpec((tm, d), lambda i: (i, 0))
    return pl.pallas_call(
        _outproj_kernel,
        grid=(n // tm,),
        in_specs=[row(WIDTH), row(WIDTH), row(D_MODEL),
                  _const_spec((WIDTH, D_MODEL)), _const_spec((WIDTH, D_MODEL)),
                  _const_spec((1, D_MODEL)), _const_spec(wq_bf.shape)],
        out_specs=[row(D_MODEL), row(D_MODEL), row(nq)],
        out_shape=[jax.ShapeDtypeStruct((n, D_MODEL), F32),
                   jax.ShapeDtypeStruct((n, D_MODEL), F32),
                   jax.ShapeDtypeStruct((n, nq), F32)],
        compiler_params=_params(("parallel",)),
        name="outproj",
    )(yl, oa, x, w_out_bf[:WIDTH], w_out_bf[WIDTH:], g_ffn.reshape(1, D_MODEL), wq_bf)


def _topk_kernel(qp_ref, kh_ref, kl_ref, idx_ref, g_ref, top_s, top_i, cand_s, cand_i, best_s,
                 out_g, out_i, *, tn):
    iota_k = lax.broadcasted_iota(I32, (PEER_KEYS, tn), 0)
    ncand = PEER_TOPK * PEER_TOPK
    iota_c = lax.broadcasted_iota(I32, (ncand, tn), 0)
    for h in range(PEER_HEADS):
        for c in range(2):
            hc = 2 * h + c
            q_hi, q_lo = _split(qp_ref[:, hc * PEER_HALF:(hc + 1) * PEER_HALF])
            kh = kh_ref[hc]
            s = _dot_nt(kh, q_hi) + _dot_nt(kh, q_lo) + _dot_nt(kl_ref[hc], q_hi)
            for r in range(PEER_TOPK):
                m = jnp.max(s, axis=0, keepdims=True)
                am = jnp.min(jnp.where(s == m, iota_k, PEER_KEYS), axis=0, keepdims=True)
                s = jnp.where(iota_k == am, -jnp.inf, s)
                top_s[c, r:r + 1, :] = m
                top_i[c, r:r + 1, :] = am
        s1 = top_s[1]
        i1 = top_i[1]
        for a in range(PEER_TOPK):
            rows = slice(a * PEER_TOPK, (a + 1) * PEER_TOPK)
            cand_s[rows, :] = top_s[0, a:a + 1, :] + s1
            cand_i[rows, :] = top_i[0, a:a + 1, :] * PEER_KEYS + i1
        cs = cand_s[...]
        ci = cand_i[...]
        for r in range(PEER_TOPK):
            m = jnp.max(cs, axis=0, keepdims=True)
            pos = jnp.min(jnp.where(cs == m, iota_c, ncand), axis=0, keepdims=True)
            sel = iota_c == pos
            out_i[h * PEER_TOPK + r:h * PEER_TOPK + r + 1, :] = jnp.max(
                jnp.where(sel, ci, -1), axis=0, keepdims=True)
            best_s[r:r + 1, :] = m
            cs = jnp.where(sel, -jnp.inf, cs)
        bs = best_s[...]
        e = jnp.exp(bs - bs[0:1, :])
        out_g[h * PEER_TOPK:(h + 1) * PEER_TOPK, :] = e / jnp.sum(e, axis=0, keepdims=True)
    g_ref[...] = out_g[...].T
    idx_ref[...] = out_i[...].T


def _topk(qp, sub_keys):
    n = qp.shape[0]
    tn = min(TOPK_TILE, n)
    kh, kl = _split(sub_keys.reshape(PEER_HEADS * 2, PEER_KEYS, PEER_HALF))
    kern = functools.partial(_topk_kernel, tn=tn)
    ncand = PEER_TOPK * PEER_TOPK
    return pl.pallas_call(
        kern,
        grid=(n // tn,),
        in_specs=[pl.BlockSpec((tn, qp.shape[1]), lambda i: (i, 0)),
                  _const_spec(kh.shape), _const_spec(kl.shape)],
        out_specs=[pl.BlockSpec((tn, PEER_SEL), lambda i: (i, 0))] * 2,
        out_shape=[jax.ShapeDtypeStruct((n, PEER_SEL), I32),
                   jax.ShapeDtypeStruct((n, PEER_SEL), F32)],
        scratch_shapes=[pltpu.VMEM((2, PEER_TOPK, tn), F32),
                        pltpu.VMEM((2, PEER_TOPK, tn), I32),
                        pltpu.VMEM((ncand, tn), F32),
                        pltpu.VMEM((ncand, tn), I32),
                        pltpu.VMEM((PEER_TOPK, tn), F32),
                        pltpu.VMEM((PEER_SEL, tn), F32),
                        pltpu.VMEM((PEER_SEL, tn), I32)],
        compiler_params=_params(("parallel",)),
        name="peer_topk",
    )(qp, kh, kl)


def _peer_kernel(idx_ref, nxt_ref, xn_ref, g_ref, tab_ref, f_ref, buf, sem, a_s, c_s, *, tb):
    s = pl.program_id(0)
    n = pl.num_programs(0)
    slot = lax.rem(s, 2)
    other = 1 - slot
    ncol = PEER_SEL * SUBLANES

    def row_copy(e, slot_, t, r):
        return pltpu.make_async_copy(tab_ref.at[e], buf.at[slot_, t, r], sem.at[slot_, t])

    def issue(ids, slot_, t, r0, r1):
        for r in range(r0, r1):
            row_copy(ids[t, r], slot_, t, r).start(priority=r % 2)

    def drain(slot_, t):
        for r in range(PEER_SEL):
            row_copy(0, slot_, t, r).wait()

    @pl.when(s == 0)
    def _():
        def tok(t, carry):
            issue(idx_ref, 0, t, 0, PEER_SEL)
            return carry
        lax.fori_loop(0, tb, tok, 0)

    diag = (lax.broadcasted_iota(I32, (SUBLANES, ncol), 1) & (SUBLANES - 1)) == \
        lax.broadcasted_iota(I32, (SUBLANES, ncol), 0)

    for t in range(tb):
        drain(slot, t)
        x8 = xn_ref[t].astype(BF16)
        u2 = buf[slot, t, :, 0:SUBLANES, :].reshape(ncol, LANES).astype(BF16)
        m = _dot_nt(x8, u2)
        a_s[t:t + 1, :] = jnp.sum(jnp.where(diag, m, 0.0), axis=0, keepdims=True)
        issue(nxt_ref, other, t, 0, PEER_SEL // 2)

    lane = lax.broadcasted_iota(I32, (tb, LANES), 1)
    cols = []
    for c in range(ncol // LANES):
        x = a_s[:, c * LANES:(c + 1) * LANES]
        for k in (1, 2, 4):
            x = x + jnp.where((lane & k) != 0, pltpu.roll(x, k, axis=1),
                              pltpu.roll(x, LANES - k, axis=1))
        cols.append(x)
    act = jnp.concatenate(cols, axis=1)
    expand = (lax.broadcasted_iota(I32, (PEER_SEL, ncol), 1) >> 3) == \
        lax.broadcasted_iota(I32, (PEER_SEL, ncol), 0)
    expand = jnp.where(expand, 1.0, 0.0).astype(BF16)
    g_hi, g_lo = _split(g_ref[...])
    g_rep = _dot(g_hi, expand) + _dot(g_lo, expand)
    c_s[...] = g_rep * jax.nn.gelu(act)

    for t in range(tb):
        c8 = jnp.where(diag, jnp.broadcast_to(c_s[t:t + 1, :], (SUBLANES, ncol)), 0.0).astype(BF16)
        v2 = buf[slot, t, :, SUBLANES:2 * SUBLANES, :].reshape(ncol, LANES).astype(BF16)
        f_ref[t] = _dot(c8, v2)
        issue(nxt_ref, other, t, PEER_SEL // 2, PEER_SEL)

    @pl.when(s == n - 1)
    def _():
        def tok(t, carry):
            drain(other, t)
            return carry
        lax.fori_loop(0, tb, tok, 0)


def _peer(idx, g, xn, table):
    n = xn.shape[0]
    tb = PEER_TILE
    steps = n // tb
    kern = functools.partial(_peer_kernel, tb=tb)
    smem = lambda fn: pl.BlockSpec((tb, PEER_SEL), fn, memory_space=pltpu.SMEM)
    slab = pl.BlockSpec((tb, SUBLANES, LANES), lambda s: (s, 0, 0))
    out = pl.pallas_call(
        kern,
        grid=(steps,),
        in_specs=[smem(lambda s: (s, 0)),
                  smem(lambda s: (jnp.minimum(s + 1, steps - 1), 0)),
                  slab,
                  pl.BlockSpec((tb, PEER_SEL), lambda s: (s, 0)),
                  pl.BlockSpec(memory_space=pl.ANY)],
        out_specs=slab,
        out_shape=jax.ShapeDtypeStruct((n, SUBLANES, LANES), F32),
        scratch_shapes=[pltpu.VMEM((2, tb, PEER_SEL, 2 * SUBLANES, LANES), F32),
                        pltpu.SemaphoreType.DMA((2, tb)),
                        pltpu.VMEM((tb, PEER_SEL * SUBLANES), F32),
                        pltpu.VMEM((tb, PEER_SEL * SUBLANES), F32)],
        compiler_params=_params(("arbitrary",)),
        name="peer_gather",
    )(idx, idx, xn.reshape(n, SUBLANES, LANES), g, table)
    return out.reshape(n, D_MODEL)


def _final_kernel(h_ref, f_ref, p_ref, gp_ref, wg_ref, wp_ref, gf_ref, y_ref):
    h = h_ref[...] + f_ref[...]
    gate = jax.nn.sigmoid(_dot(_rms(h, gp_ref[...]).astype(BF16), wg_ref[...]))
    h = h + gate * _dot(p_ref[...].astype(BF16), wp_ref[...])
    y_ref[...] = _rms(h, gf_ref[...])


def _final(h, f, p, g_ple, wg_bf, wp_bf, g_final):
    n = h.shape[0]
    tm = min(TOKEN_TILE, n)
    row = lambda d: pl.BlockSpec((tm, d), lambda i: (i, 0))
    return pl.pallas_call(
        _final_kernel,
        grid=(n // tm,),
        in_specs=[row(D_MODEL), row(D_MODEL), row(p.shape[1]), _const_spec((1, D_MODEL)),
                  _const_spec(wg_bf.shape), _const_spec(wp_bf.shape), _const_spec((1, D_MODEL))],
        out_specs=row(D_MODEL),
        out_shape=jax.ShapeDtypeStruct((n, D_MODEL), F32),
        compiler_params=_params(("parallel",)),
        name="ple_final",
    )(h, f, p, g_ple.reshape(1, D_MODEL), wg_bf, wp_bf, g_final.reshape(1, D_MODEL))


def _trunk(x, p, conv_state, lru_state, win_k, win_v, w):
    bsz, t, _ = x.shape
    n = bsz * t
    xb, gate, q, k, v, kb, vb = _inproj(x.reshape(n, D_MODEL), w["g_mix"], w["w_in"])
    seq = lambda a: a.reshape(bsz, t, WIDTH)
    xb3 = seq(xb)
    if conv_state is None:
        conv_state = jnp.zeros((bsz, 3, WIDTH), F32)
        lru_state = jnp.zeros((bsz, WIDTH), F32)
    yl, h_last = _lru(xb3, seq(gate), conv_state, lru_state, w["conv_w"], w["conv_b"],
                      w["w_ra"], w["b_ra"], w["w_ri"], w["b_ri"], w["lam"], w["g_out_lru"])
    if win_k is None:
        oa = _attn_prompt(seq(q), seq(kb), seq(vb), w["g_out_attn"])
        keep = min(MAX_WINDOW, t)
        new_k, new_v = seq(k)[:, t - keep:], seq(v)[:, t - keep:]
    else:
        past = win_k.shape[1]
        oa = _attn_sample(seq(q), win_k.reshape(bsz, past, WIDTH), win_v.reshape(bsz, past, WIDTH),
                          seq(kb), seq(vb), w["g_out_attn"])
        new_k, new_v = seq(k), seq(v)
    h1, xn2, qp = _outproj(yl.reshape(n, WIDTH), oa.reshape(n, WIDTH), x.reshape(n, D_MODEL),
                           w["w_out"], w["g_ffn"], w["w_peer_q"])
    idx, g = _topk(qp, w["sub_keys"])
    f = _peer(idx, g, xn2, w["table"])
    y = _final(h1, f, p.reshape(n, p.shape[-1]), w["g_ple"], w["w_ple_gate"], w["w_ple_proj"],
               w["g_final"])
    heads = lambda a: a.reshape(1, bsz, a.shape[1], ATTN_HEADS, HEAD_DIM)
    return (y.reshape(bsz, t, D_MODEL), xb3[:, t - 3:][None], h_last.reshape(1, bsz, WIDTH),
            heads(new_k), heads(new_v))


def kernel(x_prompt, x_sample, state_conv, state_lru, cache_win_k, cache_win_v, p_prompt, p_sample,
           g_mix, w_in, conv_w, conv_b, w_ra, b_ra, w_ri, b_ri, lru_lambda, g_out_lru, g_out_attn,
           w_out, g_ffn, w_peer_q, peer_sub_keys, peer_u, peer_v, g_ple, w_ple_gate, w_ple_proj,
           g_final):
    w = {
        "g_mix": g_mix[0].reshape(1, D_MODEL), "w_in": w_in[0].astype(BF16),
        "conv_w": conv_w[0], "conv_b": conv_b[0], "w_ra": w_ra[0], "b_ra": b_ra[0],
        "w_ri": w_ri[0], "b_ri": b_ri[0], "lam": lru_lambda[0],
        "g_out_lru": g_out_lru[0], "g_out_attn": g_out_attn[0],
        "w_out": w_out[0].astype(BF16), "g_ffn": g_ffn[0], "w_peer_q": w_peer_q[0].astype(BF16),
        "sub_keys": peer_sub_keys[0],
        "table": jnp.concatenate([peer_u[0].reshape(-1, SUBLANES, LANES),
                                  peer_v[0].reshape(-1, SUBLANES, LANES)], axis=1),
        "g_ple": g_ple[0], "w_ple_gate": w_ple_gate[0].astype(BF16),
        "w_ple_proj": w_ple_proj[0].astype(BF16), "g_final": g_final,
    }
    yp, pc, plru, pk, pv = _trunk(x_prompt, p_prompt[0], None, None, None, None, w)
    ys, sc, slru, sk, sv = _trunk(x_sample, p_sample[0], state_conv[0], state_lru[0],
                                  cache_win_k[0], cache_win_v[0], w)
    return (yp, ys, pc, plru, pk, pv, sc, slru, sk, sv)
```

```python
import functools

import jax
import jax.numpy as jnp
from jax import lax
from jax.experimental import pallas as pl
from jax.experimental.pallas import tpu as pltpu

F32 = jnp.float32
BF16 = jnp.bfloat16
I32 = jnp.int32

EPS = 1e-6
D_MODEL = 1024
HEAD_DIM = 64
ATTN_HEADS = 8
WIDTH = 512
LRU_C = 8.0
DILATED_GROUPS = ((128, 1), (512, 4), (2048, 16))
MAX_WINDOW = 2048
PEER_HEADS = 8
PEER_KEYS = 128
PEER_HALF = 128
PEER_TOPK = 16
PEER_SEL = PEER_HEADS * PEER_TOPK
NEG = -1e30

LANES = 128
SUBLANES = 8
VMEM_LIMIT = 48 * 1024 * 1024

TOKEN_TILE = 512
LRU_TILE = 256
ATTN_TILE = 256
TOPK_TILE = 128
PEER_TILE = 16


def _params(sem):
    return pltpu.CompilerParams(dimension_semantics=sem, vmem_limit_bytes=VMEM_LIMIT)


def _rms(x, g):
    return x * lax.rsqrt(jnp.mean(x * x, axis=-1, keepdims=True) + EPS) * g


def _dot(a, b):
    return jnp.dot(a, b, preferred_element_type=F32)


def _dot_nt(a, b):
    return lax.dot_general(a, b, (((1,), (1,)), ((), ())), preferred_element_type=F32)


def _split(a):
    hi = a.astype(BF16)
    lo = (a - hi.astype(F32)).astype(BF16)
    return hi, lo


def _const_spec(shape):
    nd = len(shape)
    return pl.BlockSpec(shape, lambda *_: (0,) * nd)


def _inproj_kernel(x_ref, g_ref, w_ref, xb_ref, gate_ref, q_ref, k_ref, v_ref, kb_ref, vb_ref):
    xn = _rms(x_ref[...], g_ref[...]).astype(BF16)

    def proj(c):
        return _dot(xn, w_ref[:, c * WIDTH:(c + 1) * WIDTH])

    xb_ref[...] = proj(0)
    gate_ref[...] = proj(1)
    q_ref[...] = (proj(2) * (HEAD_DIM ** -0.5)).astype(BF16)
    k = proj(3)
    k_ref[...] = k
    kb_ref[...] = k.astype(BF16)
    v = proj(4)
    v_ref[...] = v
    vb_ref[...] = v.astype(BF16)


def _inproj(x, g, w_bf):
    n = x.shape[0]
    tm = min(TOKEN_TILE, n)
    row = lambda d: pl.BlockSpec((tm, d), lambda i: (i, 0))
    f = jax.ShapeDtypeStruct((n, WIDTH), F32)
    b = jax.ShapeDtypeStruct((n, WIDTH), BF16)
    return pl.pallas_call(
        _inproj_kernel,
        grid=(n // tm,),
        in_specs=[row(D_MODEL), _const_spec((1, D_MODEL)), _const_spec(w_bf.shape)],
        out_specs=[row(WIDTH)] * 7,
        out_shape=[f, f, b, f, f, b, b],
        compiler_params=_params(("parallel",)),
        name="inproj",
    )(x, g, w_bf)


def _lru_kernel(xb_ref, gate_ref, cs_ref, h0_ref, cw_ref, cb_ref, wah_ref, wal_ref, ba_ref,
                wih_ref, wil_ref, bi_ref, lam_ref, g_ref, yl_ref, hl_ref,
                xpad, hc, a_s, b_s, h_s, *, tb):
    j = pl.program_id(1)

    @pl.when(j == 0)
    def _():
        xpad[0:SUBLANES, :] = jnp.zeros((SUBLANES, WIDTH), F32)
        xpad[SUBLANES - 3:SUBLANES, :] = cs_ref[...]
        hc[...] = jnp.broadcast_to(h0_ref[...], (SUBLANES, WIDTH))

    x = xb_ref[...]
    xpad[SUBLANES:SUBLANES + tb, :] = x
    w = cw_ref[...]
    xc = cb_ref[...] + xpad[SUBLANES - 3:SUBLANES - 3 + tb, :] * w[0:1]
    xc = xc + xpad[SUBLANES - 2:SUBLANES - 2 + tb, :] * w[1:2]
    xc = xc + xpad[SUBLANES - 1:SUBLANES - 1 + tb, :] * w[2:3]
    xc = xc + x * w[3:4]
    xpad[0:SUBLANES, :] = xpad[tb:tb + SUBLANES, :]

    x_hi, x_lo = _split(xc)

    def gate_dot(wh_ref, wl_ref):
        wh = wh_ref[...]
        return _dot(x_hi, wh) + _dot(x_lo, wh) + _dot(x_hi, wl_ref[...])

    r = jax.nn.sigmoid(gate_dot(wah_ref, wal_ref) + ba_ref[...])
    i = jax.nn.sigmoid(gate_dot(wih_ref, wil_ref) + bi_ref[...])
    nl = -lam_ref[...]
    softplus = jnp.maximum(nl, 0.0) + jnp.log1p(jnp.exp(-jnp.abs(nl)))
    log_a = -LRU_C * r * softplus
    a = jnp.exp(log_a)
    a_s[...] = a
    b_s[...] = jnp.sqrt(jnp.tanh(-log_a) * (a * a + 1.0)) * (i * xc)

    rows = lax.broadcasted_iota(I32, (SUBLANES, WIDTH), 0)

    def group(gi, hb):
        off = pl.multiple_of(gi * SUBLANES, SUBLANES)
        av = a_s[pl.ds(off, SUBLANES), :]
        bv = b_s[pl.ds(off, SUBLANES), :]
        for d in (1, 2, 4):
            a_sh = pltpu.roll(av, d, axis=0)
            b_sh = pltpu.roll(bv, d, axis=0)
            m = rows >= d
            bv = jnp.where(m, av * b_sh + bv, bv)
            av = jnp.where(m, av * a_sh, av)
        h = av * hb + bv
        h_s[pl.ds(off, SUBLANES), :] = h
        return jnp.broadcast_to(h[SUBLANES - 1:SUBLANES, :], (SUBLANES, WIDTH))

    hb = lax.fori_loop(0, tb // SUBLANES, group, hc[...])
    hc[...] = hb

    y = h_s[...] * jax.nn.gelu(gate_ref[...])
    yl_ref[...] = _rms(y, g_ref[...]).astype(BF16)

    @pl.when(j == pl.num_programs(1) - 1)
    def _():
        hl_ref[...] = hb[0:1, :]


def _lru(xb, gate, conv_state, h0, conv_w, conv_b, w_ra, b_ra, w_ri, b_ri, lam, g_out):
    bsz, t, _ = xb.shape
    tb = min(LRU_TILE, t)
    wah, wal = _split(jax.scipy.linalg.block_diag(*w_ra))
    wih, wil = _split(jax.scipy.linalg.block_diag(*w_ri))
    seq = pl.BlockSpec((None, tb, WIDTH), lambda b, j: (b, j, 0))
    vec = _const_spec((1, WIDTH))
    sq = _const_spec((WIDTH, WIDTH))
    kern = functools.partial(_lru_kernel, tb=tb)
    return pl.pallas_call(
        kern,
        grid=(bsz, t // tb),
        in_specs=[seq, seq,
                  pl.BlockSpec((None, 3, WIDTH), lambda b, j: (b, 0, 0)),
                  pl.BlockSpec((None, 1, WIDTH), lambda b, j: (b, 0, 0)),
                  _const_spec((4, WIDTH)), vec, sq, sq, vec, sq, sq, vec, vec, vec],
        out_specs=[seq, pl.BlockSpec((None, 1, WIDTH), lambda b, j: (b, 0, 0))],
        out_shape=[jax.ShapeDtypeStruct((bsz, t, WIDTH), BF16),
                   jax.ShapeDtypeStruct((bsz, 1, WIDTH), F32)],
        scratch_shapes=[pltpu.VMEM((tb + SUBLANES, WIDTH), F32),
                        pltpu.VMEM((SUBLANES, WIDTH), F32),
                        pltpu.VMEM((tb, WIDTH), F32),
                        pltpu.VMEM((tb, WIDTH), F32),
                        pltpu.VMEM((tb, WIDTH), F32)],
        compiler_params=_params(("parallel", "arbitrary")),
        name="rg_lru",
    )(xb, gate, conv_state, h0.reshape(bsz, 1, WIDTH), conv_w, conv_b.reshape(1, WIDTH),
      wah, wal, b_ra.reshape(1, WIDTH), wih, wil, b_ri.reshape(1, WIDTH),
      lam.reshape(1, WIDTH), g_out.reshape(1, WIDTH))


def _tap_count(d):
    one = jnp.ones(d.shape, F32)
    zero = jnp.zeros(d.shape, F32)
    cnt = zero
    for window, dil in DILATED_GROUPS:
        hit = jnp.where(d <= window, one, zero)
        if dil > 1:
            hit = jnp.where((d & (dil - 1)) == 0, hit, zero)
        cnt = cnt + hit
    return jnp.where(d >= 0, cnt, zero)


def _pair_masks(rows):
    lane = lax.broadcasted_iota(I32, (rows, LANES), 1)
    return lane < HEAD_DIM


def _attn_kernel(q_ref, k_ref, v_ref, g_ref, o_ref, m_s, l_s, acc_s, *, tq, nkb):
    i = pl.program_id(1)
    j = pl.program_id(2)

    @pl.when(j == 0)
    def _():
        m_s[...] = jnp.full(m_s.shape, NEG, F32)
        l_s[...] = jnp.zeros(l_s.shape, F32)
        acc_s[...] = jnp.zeros(acc_s.shape, F32)

    kb = i - (nkb - 1) + j
    lo = _pair_masks(tq)

    @pl.when(kb >= 0)
    def _():
        r = lax.broadcasted_iota(I32, (tq, tq), 0)
        c = lax.broadcasted_iota(I32, (tq, tq), 1)
        cnt = _tap_count((i - kb) * tq + r - c)
        valid = cnt > 0.0
        zb = jnp.zeros((tq, LANES), BF16)
        for hp in range(ATTN_HEADS // 2):
            sl = slice(hp * LANES, (hp + 1) * LANES)
            qp = q_ref[:, sl]
            kp = k_ref[:, sl]
            vp = v_ref[:, sl]
            qs = (jnp.where(lo, qp, zb), jnp.where(lo, zb, qp))
            ps, alphas = [], []
            for hh in range(2):
                h = 2 * hp + hh
                s = jnp.where(valid, _dot_nt(qs[hh], kp), NEG)
                m_old = m_s[h][:, 0:1]
                m_new = jnp.maximum(m_old, jnp.max(s, axis=-1, keepdims=True))
                p = jnp.exp(s - m_new) * cnt
                alpha = jnp.exp(m_old - m_new)
                l_new = alpha * l_s[h][:, 0:1] + jnp.sum(p, axis=-1, keepdims=True)
                m_s[h] = jnp.broadcast_to(m_new, (tq, LANES))
                l_s[h] = jnp.broadcast_to(l_new, (tq, LANES))
                ps.append(p.astype(BF16))
                alphas.append(alpha)
            pcat = jnp.concatenate(ps, axis=1)
            vbd = jnp.concatenate([jnp.where(lo, vp, zb), jnp.where(lo, zb, vp)], axis=0)
            alpha_pair = jnp.where(lo, alphas[0], alphas[1])
            acc_s[:, sl] = alpha_pair * acc_s[:, sl] + _dot(pcat, vbd)

    @pl.when(j == nkb - 1)
    def _():
        parts = []
        for hp in range(ATTN_HEADS // 2):
            sl = slice(hp * LANES, (hp + 1) * LANES)
            l_pair = jnp.where(lo, l_s[2 * hp][:, 0:1], l_s[2 * hp + 1][:, 0:1])
            parts.append(acc_s[:, sl] / l_pair)
        o = jnp.concatenate(parts, axis=1)
        o_ref[...] = _rms(o, g_ref[...]).astype(BF16)


def _attn_prompt(q, k, v, g):
    bsz, t, _ = q.shape
    tq = min(ATTN_TILE, t)
    nkb = MAX_WINDOW // tq + 1
    qspec = pl.BlockSpec((None, tq, WIDTH), lambda b, i, j: (b, i, 0))
    kspec = pl.BlockSpec((None, tq, WIDTH), lambda b, i, j: (b, jnp.maximum(i - (nkb - 1) + j, 0), 0))
    kern = functools.partial(_attn_kernel, tq=tq, nkb=nkb)
    return pl.pallas_call(
        kern,
        grid=(bsz, t // tq, nkb),
        in_specs=[qspec, kspec, kspec, _const_spec((1, WIDTH))],
        out_specs=qspec,
        out_shape=jax.ShapeDtypeStruct((bsz, t, WIDTH), BF16),
        scratch_shapes=[pltpu.VMEM((ATTN_HEADS, tq, LANES), F32),
                        pltpu.VMEM((ATTN_HEADS, tq, LANES), F32),
                        pltpu.VMEM((tq, WIDTH), F32)],
        compiler_params=_params(("parallel", "parallel", "arbitrary")),
        name="attn_prompt",
    )(q, k, v, g.reshape(1, WIDTH))


def _attn_sample_kernel(q_ref, kc_ref, vc_ref, kn_ref, vn_ref, g_ref, o_ref, *, t, past):
    lo_q = _pair_masks(t)
    lo_c = _pair_masks(past)
    rq = lax.broadcasted_iota(I32, (t, past), 0)
    cp = lax.broadcasted_iota(I32, (t, past), 1)
    cnt_c = _tap_count(past + rq - cp)
    r2 = lax.broadcasted_iota(I32, (t, t), 0)
    c2 = lax.broadcasted_iota(I32, (t, t), 1)
    cnt_n = _tap_count(r2 - c2)
    val_c = cnt_c > 0.0
    val_n = cnt_n > 0.0
    zq = jnp.zeros((t, LANES), BF16)
    zc = jnp.zeros((past, LANES), BF16)
    parts = []
    for hp in range(ATTN_HEADS // 2):
        sl = slice(hp * LANES, (hp + 1) * LANES)
        qp = q_ref[:, sl]
        kc = kc_ref[:, sl].astype(BF16)
        vc = vc_ref[:, sl].astype(BF16)
        kn = kn_ref[:, sl]
        vn = vn_ref[:, sl]
        qs = (jnp.where(lo_q, qp, zq), jnp.where(lo_q, zq, qp))
        pcs, pns, ls = [], [], []
        for hh in range(2):
            s_c = jnp.where(val_c, _dot_nt(qs[hh], kc), NEG)
            s_n = jnp.where(val_n, _dot_nt(qs[hh], kn), NEG)
            m = jnp.maximum(jnp.max(s_c, axis=-1, keepdims=True), jnp.max(s_n, axis=-1, keepdims=True))
            p_c = jnp.exp(s_c - m) * cnt_c
            p_n = jnp.exp(s_n - m) * cnt_n
            ls.append(jnp.sum(p_c, axis=-1, keepdims=True) + jnp.sum(p_n, axis=-1, keepdims=True))
            pcs.append(p_c.astype(BF16))
            pns.append(p_n.astype(BF16))
        vbd_c = jnp.concatenate([jnp.where(lo_c, vc, zc), jnp.where(lo_c, zc, vc)], axis=0)
        vbd_n = jnp.concatenate([jnp.where(lo_q, vn, zq), jnp.where(lo_q, zq, vn)], axis=0)
        pv = _dot(jnp.concatenate(pcs, axis=1), vbd_c) + _dot(jnp.concatenate(pns, axis=1), vbd_n)
        parts.append(pv / jnp.where(lo_q, ls[0], ls[1]))
    o = jnp.concatenate(parts, axis=1)
    o_ref[...] = _rms(o, g_ref[...]).astype(BF16)


def _attn_sample(q, k_cache, v_cache, k_new, v_new, g):
    bsz, t, _ = q.shape
    past = k_cache.shape[1]
    new = pl.BlockSpec((None, t, WIDTH), lambda b: (b, 0, 0))
    old = pl.BlockSpec((None, past, WIDTH), lambda b: (b, 0, 0))
    kern = functools.partial(_attn_sample_kernel, t=t, past=past)
    return pl.pallas_call(
        kern,
        grid=(bsz,),
        in_specs=[new, old, old, new, new, _const_spec((1, WIDTH))],
        out_specs=new,
        out_shape=jax.ShapeDtypeStruct((bsz, t, WIDTH), BF16),
        compiler_params=_params(("parallel",)),
        name="attn_sample",
    )(q, k_cache, v_cache, k_new, v_new, g.reshape(1, WIDTH))


def _outproj_kernel(yl_ref, oa_ref, x_ref, wt_ref, wb_ref, g_ref, wq_ref, h_ref, xn_ref, qp_ref):
    h = x_ref[...] + (_dot(yl_ref[...], wt_ref[...]) + _dot(oa_ref[...], wb_ref[...]))
    h_ref[...] = h
    xn = _rms(h, g_ref[...])
    xn_ref[...] = xn
    qp_ref[...] = _dot(xn.astype(BF16), wq_ref[...])


def _outproj(yl, oa, x, w_out_bf, g_ffn, wq_bf):
    n = x.shape[0]
    tm = min(TOKEN_TILE, n)
    nq = wq_bf.shape[1]
    row = lambda d: pl.BlockSpec((tm, d), lambda i: (i, 0))
    return pl.pallas_call(
        _outproj_kernel,
        grid=(n // tm,),
        in_specs=[row(WIDTH), row(WIDTH), row(D_MODEL),
                  _const_spec((WIDTH, D_MODEL)), _const_spec((WIDTH, D_MODEL)),
                  _const_spec((1, D_MODEL)), _const_spec(wq_bf.shape)],
        out_specs=[row(D_MODEL), row(D_MODEL), row(nq)],
        out_shape=[jax.ShapeDtypeStruct((n, D_MODEL), F32),
                   jax.ShapeDtypeStruct((n, D_MODEL), F32),
                   jax.ShapeDtypeStruct((n, nq), F32)],
        compiler_params=_params(("parallel",)),
        name="outproj",
    )(yl, oa, x, w_out_bf[:WIDTH], w_out_bf[WIDTH:], g_ffn.reshape(1, D_MODEL), wq_bf)


def _first_max(vals, tags, extra=None):
    pay = [tags] + ([extra] if extra is not None else [])
    while len(vals) > 1:
        nv, npay = [], [[] for _ in pay]
        for a in range(0, len(vals) - 1, 2):
            take_b = vals[a + 1] > vals[a]
            nv.append(jnp.maximum(vals[a], vals[a + 1]))
            for k, p in enumerate(pay):
                npay[k].append(jnp.where(take_b, p[a + 1], p[a]))
        if len(vals) % 2:
            nv.append(vals[-1])
            for k, p in enumerate(pay):
                npay[k].append(p[-1])
        vals, pay = nv, npay
    v = vals[0]
    pay = [p[0] for p in pay]
    for d in (4, 2, 1):
        vr = pltpu.roll(v, d, axis=0)
        pr = [pltpu.roll(p, d, axis=0) for p in pay]
        better = (vr > v) | ((vr == v) & (pr[0] < pay[0]))
        v = jnp.where(better, vr, v)
        pay = [jnp.where(better, a, b) for a, b in zip(pr, pay)]
    return (v, *pay)


def _topk_kernel(qp_ref, kh_ref, kl_ref, idx_ref, g_ref, top_s, top_i, best_s, out_g, out_i, *, tn):
    sub = lax.broadcasted_iota(I32, (SUBLANES, tn), 0)
    subf = sub.astype(F32)
    nslab = PEER_KEYS // SUBLANES
    key_tags = [subf + float(SUBLANES * k) for k in range(nslab)]
    ninf = jnp.full((SUBLANES, tn), -jnp.inf, F32)
    for h in range(PEER_HEADS):
        for c in range(2):
            hc = 2 * h + c
            q_hi, q_lo = _split(qp_ref[:, hc * PEER_HALF:(hc + 1) * PEER_HALF])
            kh = kh_ref[hc]
            s = _dot_nt(kh, q_hi) + _dot_nt(kh, q_lo) + _dot_nt(kl_ref[hc], q_hi)
            slabs = [s[SUBLANES * k:SUBLANES * (k + 1), :] for k in range(nslab)]
            for r in range(PEER_TOPK):
                m, am = _first_max(slabs, key_tags)
                slabs = [jnp.where(t == am, ninf, x) for x, t in zip(slabs, key_tags)]
                top_s[c, r:r + 1, :] = m[0:1, :]
                top_i[c, r:r + 1, :] = am[0:1, :]
        s0_lo, s0_hi = top_s[0, 0:SUBLANES, :], top_s[0, SUBLANES:2 * SUBLANES, :]
        i0_lo, i0_hi = top_i[0, 0:SUBLANES, :], top_i[0, SUBLANES:2 * SUBLANES, :]
        s1_lo, s1_hi = top_s[1, 0:SUBLANES, :], top_s[1, SUBLANES:2 * SUBLANES, :]
        i1_lo, i1_hi = top_i[1, 0:SUBLANES, :], top_i[1, SUBLANES:2 * SUBLANES, :]
        bcast = lambda x, a: jnp.broadcast_to(x[a:a + 1, :], (SUBLANES, tn))
        cs = [bcast(s0_lo, 0) + s1_lo, bcast(s0_lo, 0) + s1_hi]
        ci = [bcast(i0_lo, 0) * PEER_KEYS + i1_lo, bcast(i0_lo, 0) * PEER_KEYS + i1_hi]
        pos = [subf, subf + float(SUBLANES)]
        for a in range(1, SUBLANES):
            nb = PEER_TOPK // (a + 1)
            cs.append(jnp.where(sub < nb, bcast(s0_lo, a) + s1_lo, ninf))
            ci.append(bcast(i0_lo, a) * PEER_KEYS + i1_lo)
            pos.append(subf + float(a * PEER_TOPK))
        cs.append(s0_hi + bcast(s1_lo, 0))
        ci.append(i0_hi * PEER_KEYS + bcast(i1_lo, 0))
        pos.append((subf + float(SUBLANES)) * PEER_TOPK)
        for r in range(PEER_TOPK):
            m, p, e = _first_max(cs, pos, ci)
            cs = [jnp.where(t == p, ninf, x) for x, t in zip(cs, pos)]
            row = h * PEER_TOPK + r
            out_i[row:row + 1, :] = e[0:1, :]
            best_s[r:r + 1, :] = m[0:1, :]
        bs = best_s[...]
        ex = jnp.exp(bs - bs[0:1, :])
        out_g[h * PEER_TOPK:(h + 1) * PEER_TOPK, :] = ex / jnp.sum(ex, axis=0, keepdims=True)
    g_ref[...] = out_g[...].T
    idx_ref[...] = out_i[...].T.astype(I32)


def _topk(qp, sub_keys):
    n = qp.shape[0]
    tn = min(TOPK_TILE, n)
    kh, kl = _split(sub_keys.reshape(PEER_HEADS * 2, PEER_KEYS, PEER_HALF))
    kern = functools.partial(_topk_kernel, tn=tn)
    return pl.pallas_call(
        kern,
        grid=(n // tn,),
        in_specs=[pl.BlockSpec((tn, qp.shape[1]), lambda i: (i, 0)),
                  _const_spec(kh.shape), _const_spec(kl.shape)],
        out_specs=[pl.BlockSpec((tn, PEER_SEL), lambda i: (i, 0))] * 2,
        out_shape=[jax.ShapeDtypeStruct((n, PEER_SEL), I32),
                   jax.ShapeDtypeStruct((n, PEER_SEL), F32)],
        scratch_shapes=[pltpu.VMEM((2, PEER_TOPK, tn), F32),
                        pltpu.VMEM((2, PEER_TOPK, tn), F32),
                        pltpu.VMEM((PEER_TOPK, tn), F32),
                        pltpu.VMEM((PEER_SEL, tn), F32),
                        pltpu.VMEM((PEER_SEL, tn), F32)],
        compiler_params=_params(("parallel",)),
        name="peer_topk",
    )(qp, kh, kl)


def _peer_kernel(idx_ref, nxt_ref, xn_ref, g_ref, tab_ref, f_ref, buf, sem, a_s, c_s, *, tb):
    s = pl.program_id(0)
    n = pl.num_programs(0)
    slot = lax.rem(s, 2)
    other = 1 - slot
    ncol = PEER_SEL * SUBLANES

    def row_copy(e, slot_, t, r):
        return pltpu.make_async_copy(tab_ref.at[e], buf.at[slot_, t, r], sem.at[slot_, t])

    def issue(ids, slot_, t, r0, r1):
        for r in range(r0, r1):
            row_copy(ids[t, r], slot_, t, r).start(priority=r % 2)

    def drain(slot_, t):
        for r in range(PEER_SEL):
            row_copy(0, slot_, t, r).wait()

    @pl.when(s == 0)
    def _():
        def tok(t, carry):
            issue(idx_ref, 0, t, 0, PEER_SEL)
            return carry
        lax.fori_loop(0, tb, tok, 0)

    diag = (lax.broadcasted_iota(I32, (SUBLANES, ncol), 1) & (SUBLANES - 1)) == \
        lax.broadcasted_iota(I32, (SUBLANES, ncol), 0)

    for t in range(tb):
        drain(slot, t)
        x8 = xn_ref[t].astype(BF16)
        u2 = buf[slot, t, :, 0:SUBLANES, :].reshape(ncol, LANES).astype(BF16)
        m = _dot_nt(x8, u2)
        a_s[t:t + 1, :] = jnp.sum(jnp.where(diag, m, 0.0), axis=0, keepdims=True)
        issue(nxt_ref, other, t, 0, PEER_SEL // 2)

    lane = lax.broadcasted_iota(I32, (tb, LANES), 1)
    cols = []
    for c in range(ncol // LANES):
        x = a_s[:, c * LANES:(c + 1) * LANES]
        for k in (1, 2, 4):
            x = x + jnp.where((lane & k) != 0, pltpu.roll(x, k, axis=1),
                              pltpu.roll(x, LANES - k, axis=1))
        cols.append(x)
    act = jnp.concatenate(cols, axis=1)
    expand = (lax.broadcasted_iota(I32, (PEER_SEL, ncol), 1) >> 3) == \
        lax.broadcasted_iota(I32, (PEER_SEL, ncol), 0)
    expand = jnp.where(expand, 1.0, 0.0).astype(BF16)
    g_hi, g_lo = _split(g_ref[...])
    g_rep = _dot(g_hi, expand) + _dot(g_lo, expand)
    c_s[...] = g_rep * jax.nn.gelu(act)

    for t in range(tb):
        c8 = jnp.where(diag, jnp.broadcast_to(c_s[t:t + 1, :], (SUBLANES, ncol)), 0.0).astype(BF16)
        v2 = buf[slot, t, :, SUBLANES:2 * SUBLANES, :].reshape(ncol, LANES).astype(BF16)
        f_ref[t] = _dot(c8, v2)
        issue(nxt_ref, other, t, PEER_SEL // 2, PEER_SEL)

    @pl.when(s == n - 1)
    def _():
        def tok(t, carry):
            drain(other, t)
            return carry
        lax.fori_loop(0, tb, tok, 0)


def _peer(idx, g, xn, table):
    n = xn.shape[0]
    tb = PEER_TILE
    steps = n // tb
    kern = functools.partial(_peer_kernel, tb=tb)
    smem = lambda fn: pl.BlockSpec((tb, PEER_SEL), fn, memory_space=pltpu.SMEM)
    slab = pl.BlockSpec((tb, SUBLANES, LANES), lambda s: (s, 0, 0))
    out = pl.pallas_call(
        kern,
        grid=(steps,),
        in_specs=[smem(lambda s: (s, 0)),
                  smem(lambda s: (jnp.minimum(s + 1, steps - 1), 0)),
                  slab,
                  pl.BlockSpec((tb, PEER_SEL), lambda s: (s, 0)),
                  pl.BlockSpec(memory_space=pl.ANY)],
        out_specs=slab,
        out_shape=jax.ShapeDtypeStruct((n, SUBLANES, LANES), F32),
        scratch_shapes=[pltpu.VMEM((2, tb, PEER_SEL, 2 * SUBLANES, LANES), F32),
                        pltpu.SemaphoreType.DMA((2, tb)),
                        pltpu.VMEM((tb, PEER_SEL * SUBLANES), F32),
                        pltpu.VMEM((tb, PEER_SEL * SUBLANES), F32)],
        compiler_params=_params(("arbitrary",)),
        name="peer_gather",
    )(idx, idx, xn.reshape(n, SUBLANES, LANES), g, table)
    return out.reshape(n, D_MODEL)


def _final_kernel(h_ref, f_ref, p_ref, gp_ref, wg_ref, wp_ref, gf_ref, y_ref):
    h = h_ref[...] + f_ref[...]
    gate = jax.nn.sigmoid(_dot(_rms(h, gp_ref[...]).astype(BF16), wg_ref[...]))
    h = h + gate * _dot(p_ref[...].astype(BF16), wp_ref[...])
    y_ref[...] = _rms(h, gf_ref[...])


def _final(h, f, p, g_ple, wg_bf, wp_bf, g_final):
    n = h.shape[0]
    tm = min(TOKEN_TILE, n)
    row = lambda d: pl.BlockSpec((tm, d), lambda i: (i, 0))
    return pl.pallas_call(
        _final_kernel,
        grid=(n // tm,),
        in_specs=[row(D_MODEL), row(D_MODEL), row(p.shape[1]), _const_spec((1, D_MODEL)),
                  _const_spec(wg_bf.shape), _const_spec(wp_bf.shape), _const_spec((1, D_MODEL))],
        out_specs=row(D_MODEL),
        out_shape=jax.ShapeDtypeStruct((n, D_MODEL), F32),
        compiler_params=_params(("parallel",)),
        name="ple_final",
    )(h, f, p, g_ple.reshape(1, D_MODEL), wg_bf, wp_bf, g_final.reshape(1, D_MODEL))


def _trunk(x, p, conv_state, lru_state, win_k, win_v, w):
    bsz, t, _ = x.shape
    n = bsz * t
    xb, gate, q, k, v, kb, vb = _inproj(x.reshape(n, D_MODEL), w["g_mix"], w["w_in"])
    seq = lambda a: a.reshape(bsz, t, WIDTH)
    xb3 = seq(xb)
    if conv_state is None:
        conv_state = jnp.zeros((bsz, 3, WIDTH), F32)
        lru_state = jnp.zeros((bsz, WIDTH), F32)
    yl, h_last = _lru(xb3, seq(gate), conv_state, lru_state, w["conv_w"], w["conv_b"],
                      w["w_ra"], w["b_ra"], w["w_ri"], w["b_ri"], w["lam"], w["g_out_lru"])
    if win_k is None:
        oa = _attn_prompt(seq(q), seq(kb), seq(vb), w["g_out_attn"])
        keep = min(MAX_WINDOW, t)
        new_k, new_v = seq(k)[:, t - keep:], seq(v)[:, t - keep:]
    else:
        past = win_k.shape[1]
        oa = _attn_sample(seq(q), win_k.reshape(bsz, past, WIDTH), win_v.reshape(bsz, past, WIDTH),
                          seq(kb), seq(vb), w["g_out_attn"])
        new_k, new_v = seq(k), seq(v)
    h1, xn2, qp = _outproj(yl.reshape(n, WIDTH), oa.reshape(n, WIDTH), x.reshape(n, D_MODEL),
                           w["w_out"], w["g_ffn"], w["w_peer_q"])
    idx, g = _topk(qp, w["sub_keys"])
    f = _peer(idx, g, xn2, w["table"])
    y = _final(h1, f, p.reshape(n, p.shape[-1]), w["g_ple"], w["w_ple_gate"], w["w_ple_proj"],
               w["g_final"])
    heads = lambda a: a.reshape(1, bsz, a.shape[1], ATTN_HEADS, HEAD_DIM)
    return (y.reshape(bsz, t, D_MODEL), xb3[:, t - 3:][None], h_last.reshape(1, bsz, WIDTH),
            heads(new_k), heads(new_v))


def kernel(x_prompt, x_sample, state_conv, state_lru, cache_win_k, cache_win_v, p_prompt, p_sample,
           g_mix, w_in, conv_w, conv_b, w_ra, b_ra, w_ri, b_ri, lru_lambda, g_out_lru, g_out_attn,
           w_out, g_ffn, w_peer_q, peer_sub_keys, peer_u, peer_v, g_ple, w_ple_gate, w_ple_proj,
           g_final):
    w = {
        "g_mix": g_mix[0].reshape(1, D_MODEL), "w_in": w_in[0].astype(BF16),
        "conv_w": conv_w[0], "conv_b": conv_b[0], "w_ra": w_ra[0], "b_ra": b_ra[0],
        "w_ri": w_ri[0], "b_ri": b_ri[0], "lam": lru_lambda[0],
        "g_out_lru": g_out_lru[0], "g_out_attn": g_out_attn[0],
        "w_out": w_out[0].astype(BF16), "g_ffn": g_ffn[0], "w_peer_q": w_peer_q[0].astype(BF16),
        "sub_keys": peer_sub_keys[0],
        "table": jnp.concatenate([peer_u[0].reshape(-1, SUBLANES, LANES),
                                  peer_v[0].reshape(-1, SUBLANES, LANES)], axis=1),
        "g_ple": g_ple[0], "w_ple_gate": w_ple_gate[0].astype(BF16),
        "w_ple_proj": w_ple_proj[0].astype(BF16), "g_final": g_final,
    }
    yp, pc, plru, pk, pv = _trunk(x_prompt, p_prompt[0], None, None, None, None, w)
    ys, sc, slru, sk, sv = _trunk(x_sample, p_sample[0], state_conv[0], state_lru[0],
                                  cache_win_k[0], cache_win_v[0], w)
    return (yp, ys, pc, plru, pk, pv, sc, slru, sk, sv)
```

```python
import functools

import jax
import jax.numpy as jnp
from jax import lax
from jax.experimental import pallas as pl
from jax.experimental.pallas import tpu as pltpu

F32 = jnp.float32
BF16 = jnp.bfloat16
I32 = jnp.int32

EPS = 1e-6
D_MODEL = 1024
HEAD_DIM = 64
ATTN_HEADS = 8
WIDTH = 512
LRU_C = 8.0
DILATED_GROUPS = ((128, 1), (512, 4), (2048, 16))
MAX_WINDOW = 2048
PEER_HEADS = 8
PEER_KEYS = 128
PEER_HALF = 128
PEER_TOPK = 16
PEER_SEL = PEER_HEADS * PEER_TOPK
NEG = -1e30

LANES = 128
SUBLANES = 8
VMEM_LIMIT = 48 * 1024 * 1024

TOKEN_TILE = 512
LRU_TILE = 256
ATTN_TILE = 256
TOPK_TILE = 128
PEER_TILE = 16
WAIT_GROUP = 2


def _params(sem):
    return pltpu.CompilerParams(dimension_semantics=sem, vmem_limit_bytes=VMEM_LIMIT)


def _rms(x, g):
    return x * lax.rsqrt(jnp.mean(x * x, axis=-1, keepdims=True) + EPS) * g


def _dot(a, b):
    return jnp.dot(a, b, preferred_element_type=F32)


def _dot_nt(a, b):
    return lax.dot_general(a, b, (((1,), (1,)), ((), ())), preferred_element_type=F32)


def _split(a):
    hi = a.astype(BF16)
    lo = (a - hi.astype(F32)).astype(BF16)
    return hi, lo


def _const_spec(shape):
    nd = len(shape)
    return pl.BlockSpec(shape, lambda *_: (0,) * nd)


def _inproj_kernel(x_ref, g_ref, w_ref, xb_ref, gate_ref, q_ref, k_ref, v_ref, kb_ref, vb_ref):
    xn = _rms(x_ref[...], g_ref[...]).astype(BF16)

    def proj(c):
        return _dot(xn, w_ref[:, c * WIDTH:(c + 1) * WIDTH])

    xb_ref[...] = proj(0)
    gate_ref[...] = proj(1)
    q_ref[...] = (proj(2) * (HEAD_DIM ** -0.5)).astype(BF16)
    k = proj(3)
    k_ref[...] = k
    kb_ref[...] = k.astype(BF16)
    v = proj(4)
    v_ref[...] = v
    vb_ref[...] = v.astype(BF16)


def _inproj(x, g, w_bf):
    n = x.shape[0]
    tm = min(TOKEN_TILE, n)
    row = lambda d: pl.BlockSpec((tm, d), lambda i: (i, 0))
    f = jax.ShapeDtypeStruct((n, WIDTH), F32)
    b = jax.ShapeDtypeStruct((n, WIDTH), BF16)
    return pl.pallas_call(
        _inproj_kernel,
        grid=(n // tm,),
        in_specs=[row(D_MODEL), _const_spec((1, D_MODEL)), _const_spec(w_bf.shape)],
        out_specs=[row(WIDTH)] * 7,
        out_shape=[f, f, b, f, f, b, b],
        compiler_params=_params(("parallel",)),
        name="inproj",
    )(x, g, w_bf)


def _lru_kernel(xb_ref, gate_ref, cs_ref, h0_ref, cw_ref, cb_ref, wah_ref, wal_ref, ba_ref,
                wih_ref, wil_ref, bi_ref, lam_ref, g_ref, yl_ref, hl_ref,
                xpad, hc, a_s, b_s, h_s, *, tb):
    j = pl.program_id(1)

    @pl.when(j == 0)
    def _():
        xpad[0:SUBLANES, :] = jnp.zeros((SUBLANES, WIDTH), F32)
        xpad[SUBLANES - 3:SUBLANES, :] = cs_ref[...]
        hc[...] = jnp.broadcast_to(h0_ref[...], (SUBLANES, WIDTH))

    x = xb_ref[...]
    xpad[SUBLANES:SUBLANES + tb, :] = x
    w = cw_ref[...]
    xc = cb_ref[...] + xpad[SUBLANES - 3:SUBLANES - 3 + tb, :] * w[0:1]
    xc = xc + xpad[SUBLANES - 2:SUBLANES - 2 + tb, :] * w[1:2]
    xc = xc + xpad[SUBLANES - 1:SUBLANES - 1 + tb, :] * w[2:3]
    xc = xc + x * w[3:4]
    xpad[0:SUBLANES, :] = xpad[tb:tb + SUBLANES, :]

    x_hi, x_lo = _split(xc)

    def gate_dot(wh_ref, wl_ref):
        wh = wh_ref[...]
        return _dot(x_hi, wh) + _dot(x_lo, wh) + _dot(x_hi, wl_ref[...])

    r = jax.nn.sigmoid(gate_dot(wah_ref, wal_ref) + ba_ref[...])
    i = jax.nn.sigmoid(gate_dot(wih_ref, wil_ref) + bi_ref[...])
    nl = -lam_ref[...]
    softplus = jnp.maximum(nl, 0.0) + jnp.log1p(jnp.exp(-jnp.abs(nl)))
    log_a = -LRU_C * r * softplus
    a = jnp.exp(log_a)
    a_s[...] = a
    b_s[...] = jnp.sqrt(jnp.tanh(-log_a) * (a * a + 1.0)) * (i * xc)

    rows = lax.broadcasted_iota(I32, (SUBLANES, WIDTH), 0)

    def group(gi, hb):
        off = pl.multiple_of(gi * SUBLANES, SUBLANES)
        av = a_s[pl.ds(off, SUBLANES), :]
        bv = b_s[pl.ds(off, SUBLANES), :]
        for d in (1, 2, 4):
            a_sh = pltpu.roll(av, d, axis=0)
            b_sh = pltpu.roll(bv, d, axis=0)
            m = rows >= d
            bv = jnp.where(m, av * b_sh + bv, bv)
            av = jnp.where(m, av * a_sh, av)
        h = av * hb + bv
        h_s[pl.ds(off, SUBLANES), :] = h
        return jnp.broadcast_to(h[SUBLANES - 1:SUBLANES, :], (SUBLANES, WIDTH))

    hb = lax.fori_loop(0, tb // SUBLANES, group, hc[...])
    hc[...] = hb

    y = h_s[...] * jax.nn.gelu(gate_ref[...])
    yl_ref[...] = _rms(y, g_ref[...]).astype(BF16)

    @pl.when(j == pl.num_programs(1) - 1)
    def _():
        hl_ref[...] = hb[0:1, :]


def _lru(xb, gate, conv_state, h0, conv_w, conv_b, w_ra, b_ra, w_ri, b_ri, lam, g_out):
    bsz, t, _ = xb.shape
    tb = min(LRU_TILE, t)
    wah, wal = _split(jax.scipy.linalg.block_diag(*w_ra))
    wih, wil = _split(jax.scipy.linalg.block_diag(*w_ri))
    seq = pl.BlockSpec((None, tb, WIDTH), lambda b, j: (b, j, 0))
    vec = _const_spec((1, WIDTH))
    sq = _const_spec((WIDTH, WIDTH))
    kern = functools.partial(_lru_kernel, tb=tb)
    return pl.pallas_call(
        kern,
        grid=(bsz, t // tb),
        in_specs=[seq, seq,
                  pl.BlockSpec((None, 3, WIDTH), lambda b, j: (b, 0, 0)),
                  pl.BlockSpec((None, 1, WIDTH), lambda b, j: (b, 0, 0)),
                  _const_spec((4, WIDTH)), vec, sq, sq, vec, sq, sq, vec, vec, vec],
        out_specs=[seq, pl.BlockSpec((None, 1, WIDTH), lambda b, j: (b, 0, 0))],
        out_shape=[jax.ShapeDtypeStruct((bsz, t, WIDTH), BF16),
                   jax.ShapeDtypeStruct((bsz, 1, WIDTH), F32)],
        scratch_shapes=[pltpu.VMEM((tb + SUBLANES, WIDTH), F32),
                        pltpu.VMEM((SUBLANES, WIDTH), F32),
                        pltpu.VMEM((tb, WIDTH), F32),
                        pltpu.VMEM((tb, WIDTH), F32),
                        pltpu.VMEM((tb, WIDTH), F32)],
        compiler_params=_params(("parallel", "arbitrary")),
        name="rg_lru",
    )(xb, gate, conv_state, h0.reshape(bsz, 1, WIDTH), conv_w, conv_b.reshape(1, WIDTH),
      wah, wal, b_ra.reshape(1, WIDTH), wih, wil, b_ri.reshape(1, WIDTH),
      lam.reshape(1, WIDTH), g_out.reshape(1, WIDTH))


def _tap_count(d):
    one = jnp.ones(d.shape, F32)
    zero = jnp.zeros(d.shape, F32)
    cnt = zero
    for window, dil in DILATED_GROUPS:
        hit = jnp.where(d <= window, one, zero)
        if dil > 1:
            hit = jnp.where((d & (dil - 1)) == 0, hit, zero)
        cnt = cnt + hit
    return jnp.where(d >= 0, cnt, zero)


def _pair_masks(rows):
    lane = lax.broadcasted_iota(I32, (rows, LANES), 1)
    return lane < HEAD_DIM


def _attn_kernel(q_ref, k_ref, v_ref, g_ref, o_ref, m_s, l_s, acc_s, *, tq, nkb):
    i = pl.program_id(1)
    j = pl.program_id(2)

    @pl.when(j == 0)
    def _():
        m_s[...] = jnp.full(m_s.shape, NEG, F32)
        l_s[...] = jnp.zeros(l_s.shape, F32)
        acc_s[...] = jnp.zeros(acc_s.shape, F32)

    kb = i - (nkb - 1) + j
    lo = _pair_masks(tq)

    @pl.when(kb >= 0)
    def _():
        r = lax.broadcasted_iota(I32, (tq, tq), 0)
        c = lax.broadcasted_iota(I32, (tq, tq), 1)
        cnt = _tap_count((i - kb) * tq + r - c)
        valid = cnt > 0.0
        zb = jnp.zeros((tq, LANES), BF16)
        for hp in range(ATTN_HEADS // 2):
            sl = slice(hp * LANES, (hp + 1) * LANES)
            qp = q_ref[:, sl]
            kp = k_ref[:, sl]
            vp = v_ref[:, sl]
            qs = (jnp.where(lo, qp, zb), jnp.where(lo, zb, qp))
            ps, alphas = [], []
            for hh in range(2):
                h = 2 * hp + hh
                s = jnp.where(valid, _dot_nt(qs[hh], kp), NEG)
                m_old = m_s[h][:, 0:1]
                m_new = jnp.maximum(m_old, jnp.max(s, axis=-1, keepdims=True))
                p = jnp.exp(s - m_new) * cnt
                alpha = jnp.exp(m_old - m_new)
                l_new = alpha * l_s[h][:, 0:1] + jnp.sum(p, axis=-1, keepdims=True)
                m_s[h] = jnp.broadcast_to(m_new, (tq, LANES))
                l_s[h] = jnp.broadcast_to(l_new, (tq, LANES))
                ps.append(p.astype(BF16))
                alphas.append(alpha)
            pcat = jnp.concatenate(ps, axis=1)
            vbd = jnp.concatenate([jnp.where(lo, vp, zb), jnp.where(lo, zb, vp)], axis=0)
            alpha_pair = jnp.where(lo, alphas[0], alphas[1])
            acc_s[:, sl] = alpha_pair * acc_s[:, sl] + _dot(pcat, vbd)

    @pl.when(j == nkb - 1)
    def _():
        parts = []
        for hp in range(ATTN_HEADS // 2):
            sl = slice(hp * LANES, (hp + 1) * LANES)
            l_pair = jnp.where(lo, l_s[2 * hp][:, 0:1], l_s[2 * hp + 1][:, 0:1])
            parts.append(acc_s[:, sl] / l_pair)
        o = jnp.concatenate(parts, axis=1)
        o_ref[...] = _rms(o, g_ref[...]).astype(BF16)


def _attn_prompt(q, k, v, g):
    bsz, t, _ = q.shape
    tq = min(ATTN_TILE, t)
    nkb = MAX_WINDOW // tq + 1
    qspec = pl.BlockSpec((None, tq, WIDTH), lambda b, i, j: (b, i, 0))
    kspec = pl.BlockSpec((None, tq, WIDTH), lambda b, i, j: (b, jnp.maximum(i - (nkb - 1) + j, 0), 0))
    kern = functools.partial(_attn_kernel, tq=tq, nkb=nkb)
    return pl.pallas_call(
        kern,
        grid=(bsz, t // tq, nkb),
        in_specs=[qspec, kspec, kspec, _const_spec((1, WIDTH))],
        out_specs=qspec,
        out_shape=jax.ShapeDtypeStruct((bsz, t, WIDTH), BF16),
        scratch_shapes=[pltpu.VMEM((ATTN_HEADS, tq, LANES), F32),
                        pltpu.VMEM((ATTN_HEADS, tq, LANES), F32),
                        pltpu.VMEM((tq, WIDTH), F32)],
        compiler_params=_params(("parallel", "parallel", "arbitrary")),
        name="attn_prompt",
    )(q, k, v, g.reshape(1, WIDTH))


def _attn_sample_kernel(q_ref, kc_ref, vc_ref, kn_ref, vn_ref, g_ref, o_ref, *, t, past):
    lo_q = _pair_masks(t)
    lo_c = _pair_masks(past)
    rq = lax.broadcasted_iota(I32, (t, past), 0)
    cp = lax.broadcasted_iota(I32, (t, past), 1)
    cnt_c = _tap_count(past + rq - cp)
    r2 = lax.broadcasted_iota(I32, (t, t), 0)
    c2 = lax.broadcasted_iota(I32, (t, t), 1)
    cnt_n = _tap_count(r2 - c2)
    val_c = cnt_c > 0.0
    val_n = cnt_n > 0.0
    zq = jnp.zeros((t, LANES), BF16)
    zc = jnp.zeros((past, LANES), BF16)
    parts = []
    for hp in range(ATTN_HEADS // 2):
        sl = slice(hp * LANES, (hp + 1) * LANES)
        qp = q_ref[:, sl]
        kc = kc_ref[:, sl].astype(BF16)
        vc = vc_ref[:, sl].astype(BF16)
        kn = kn_ref[:, sl]
        vn = vn_ref[:, sl]
        qs = (jnp.where(lo_q, qp, zq), jnp.where(lo_q, zq, qp))
        pcs, pns, ls = [], [], []
        for hh in range(2):
            s_c = jnp.where(val_c, _dot_nt(qs[hh], kc), NEG)
            s_n = jnp.where(val_n, _dot_nt(qs[hh], kn), NEG)
            m = jnp.maximum(jnp.max(s_c, axis=-1, keepdims=True), jnp.max(s_n, axis=-1, keepdims=True))
            p_c = jnp.exp(s_c - m) * cnt_c
            p_n = jnp.exp(s_n - m) * cnt_n
            ls.append(jnp.sum(p_c, axis=-1, keepdims=True) + jnp.sum(p_n, axis=-1, keepdims=True))
            pcs.append(p_c.astype(BF16))
            pns.append(p_n.astype(BF16))
        vbd_c = jnp.concatenate([jnp.where(lo_c, vc, zc), jnp.where(lo_c, zc, vc)], axis=0)
        vbd_n = jnp.concatenate([jnp.where(lo_q, vn, zq), jnp.where(lo_q, zq, vn)], axis=0)
        pv = _dot(jnp.concatenate(pcs, axis=1), vbd_c) + _dot(jnp.concatenate(pns, axis=1), vbd_n)
        parts.append(pv / jnp.where(lo_q, ls[0], ls[1]))
    o = jnp.concatenate(parts, axis=1)
    o_ref[...] = _rms(o, g_ref[...]).astype(BF16)


def _attn_sample(q, k_cache, v_cache, k_new, v_new, g):
    bsz, t, _ = q.shape
    past = k_cache.shape[1]
    new = pl.BlockSpec((None, t, WIDTH), lambda b: (b, 0, 0))
    old = pl.BlockSpec((None, past, WIDTH), lambda b: (b, 0, 0))
    kern = functools.partial(_attn_sample_kernel, t=t, past=past)
    return pl.pallas_call(
        kern,
        grid=(bsz,),
        in_specs=[new, old, old, new, new, _const_spec((1, WIDTH))],
        out_specs=new,
        out_shape=jax.ShapeDtypeStruct((bsz, t, WIDTH), BF16),
        compiler_params=_params(("parallel",)),
        name="attn_sample",
    )(q, k_cache, v_cache, k_new, v_new, g.reshape(1, WIDTH))


def _outproj_kernel(yl_ref, oa_ref, x_ref, wt_ref, wb_ref, g_ref, wq_ref, h_ref, xn_ref, qp_ref):
    h = x_ref[...] + (_dot(yl_ref[...], wt_ref[...]) + _dot(oa_ref[...], wb_ref[...]))
    h_ref[...] = h
    xn = _rms(h, g_ref[...])
    xn_ref[...] = xn
    qp_ref[...] = _dot(xn.astype(BF16), wq_ref[...])


def _outproj(yl, oa, x, w_out_bf, g_ffn, wq_bf):
    n = x.shape[0]
    tm = min(TOKEN_TILE, n)
    nq = wq_bf.shape[1]
    row = lambda d: pl.BlockSpec((tm, d), lambda i: (i, 0))
    return pl.pallas_call(
        _outproj_kernel,
        grid=(n // tm,),
        in_specs=[row(WIDTH), row(WIDTH), row(D_MODEL),
                  _const_spec((WIDTH, D_MODEL)), _const_spec((WIDTH, D_MODEL)),
                  _const_spec((1, D_MODEL)), _const_spec(wq_bf.shape)],
        out_specs=[row(D_MODEL), row(D_MODEL), row(nq)],
        out_shape=[jax.ShapeDtypeStruct((n, D_MODEL), F32),
                   jax.ShapeDtypeStruct((n, D_MODEL), F32),
                   jax.ShapeDtypeStruct((n, nq), F32)],
        compiler_params=_params(("parallel",)),
        name="outproj",
    )(yl, oa, x, w_out_bf[:WIDTH], w_out_bf[WIDTH:], g_ffn.reshape(1, D_MODEL), wq_bf)


def _first_max(vals, tags, extra=None):
    pay = [tags] + ([extra] if extra is not None else [])
    while len(vals) > 1:
        nv, npay = [], [[] for _ in pay]
        for a in range(0, len(vals) - 1, 2):
            take_b = vals[a + 1] > vals[a]
            nv.append(jnp.maximum(vals[a], vals[a + 1]))
            for k, p in enumerate(pay):
                npay[k].append(jnp.where(take_b, p[a + 1], p[a]))
        if len(vals) % 2:
            nv.append(vals[-1])
            for k, p in enumerate(pay):
                npay[k].append(p[-1])
        vals, pay = nv, npay
    v = vals[0]
    pay = [p[0] for p in pay]
    for d in (4, 2, 1):
        vr = pltpu.roll(v, d, axis=0)
        pr = [pltpu.roll(p, d, axis=0) for p in pay]
        better = (vr > v) | ((vr == v) & (pr[0] < pay[0]))
        v = jnp.where(better, vr, v)
        pay = [jnp.where(better, a, b) for a, b in zip(pr, pay)]
    return (v, *pay)


def _topk_kernel(qp_ref, kh_ref, kl_ref, idx_ref, g_ref, top_s, top_i, best_s, out_g, out_i, *, tn):
    sub = lax.broadcasted_iota(I32, (SUBLANES, tn), 0)
    subf = sub.astype(F32)
    nslab = PEER_KEYS // SUBLANES
    key_tags = [subf + float(SUBLANES * k) for k in range(nslab)]
    ninf = jnp.full((SUBLANES, tn), -jnp.inf, F32)
    for h in range(PEER_HEADS):
        for c in range(2):
            hc = 2 * h + c
            q_hi, q_lo = _split(qp_ref[:, hc * PEER_HALF:(hc + 1) * PEER_HALF])
            kh = kh_ref[hc]
            s = _dot_nt(kh, q_hi) + _dot_nt(kh, q_lo) + _dot_nt(kl_ref[hc], q_hi)
            slabs = [s[SUBLANES * k:SUBLANES * (k + 1), :] for k in range(nslab)]
            for r in range(PEER_TOPK):
                m, am = _first_max(slabs, key_tags)
                slabs = [jnp.where(t == am, ninf, x) for x, t in zip(slabs, key_tags)]
                top_s[c, r:r + 1, :] = m[0:1, :]
                top_i[c, r:r + 1, :] = am[0:1, :]
        s0_lo, s0_hi = top_s[0, 0:SUBLANES, :], top_s[0, SUBLANES:2 * SUBLANES, :]
        i0_lo, i0_hi = top_i[0, 0:SUBLANES, :], top_i[0, SUBLANES:2 * SUBLANES, :]
        s1_lo, s1_hi = top_s[1, 0:SUBLANES, :], top_s[1, SUBLANES:2 * SUBLANES, :]
        i1_lo, i1_hi = top_i[1, 0:SUBLANES, :], top_i[1, SUBLANES:2 * SUBLANES, :]
        bcast = lambda x, a: jnp.broadcast_to(x[a:a + 1, :], (SUBLANES, tn))
        cs = [bcast(s0_lo, 0) + s1_lo, bcast(s0_lo, 0) + s1_hi]
        ci = [bcast(i0_lo, 0) * PEER_KEYS + i1_lo, bcast(i0_lo, 0) * PEER_KEYS + i1_hi]
        pos = [subf, subf + float(SUBLANES)]
        for a in range(1, SUBLANES):
            nb = PEER_TOPK // (a + 1)
            cs.append(jnp.where(sub < nb, bcast(s0_lo, a) + s1_lo, ninf))
            ci.append(bcast(i0_lo, a) * PEER_KEYS + i1_lo)
            pos.append(subf + float(a * PEER_TOPK))
        cs.append(s0_hi + bcast(s1_lo, 0))
        ci.append(i0_hi * PEER_KEYS + bcast(i1_lo, 0))
        pos.append((subf + float(SUBLANES)) * PEER_TOPK)
        for r in range(PEER_TOPK):
            m, p, e = _first_max(cs, pos, ci)
            cs = [jnp.where(t == p, ninf, x) for x, t in zip(cs, pos)]
            row = h * PEER_TOPK + r
            out_i[row:row + 1, :] = e[0:1, :]
            best_s[r:r + 1, :] = m[0:1, :]
        bs = best_s[...]
        ex = jnp.exp(bs - bs[0:1, :])
        out_g[h * PEER_TOPK:(h + 1) * PEER_TOPK, :] = ex / jnp.sum(ex, axis=0, keepdims=True)
    g_ref[...] = out_g[...].T
    idx_ref[...] = out_i[...].T.astype(I32)


def _topk(qp, sub_keys):
    n = qp.shape[0]
    tn = min(TOPK_TILE, n)
    kh, kl = _split(sub_keys.reshape(PEER_HEADS * 2, PEER_KEYS, PEER_HALF))
    kern = functools.partial(_topk_kernel, tn=tn)
    return pl.pallas_call(
        kern,
        grid=(n // tn,),
        in_specs=[pl.BlockSpec((tn, qp.shape[1]), lambda i: (i, 0)),
                  _const_spec(kh.shape), _const_spec(kl.shape)],
        out_specs=[pl.BlockSpec((tn, PEER_SEL), lambda i: (i, 0))] * 2,
        out_shape=[jax.ShapeDtypeStruct((n, PEER_SEL), I32),
                   jax.ShapeDtypeStruct((n, PEER_SEL), F32)],
        scratch_shapes=[pltpu.VMEM((2, PEER_TOPK, tn), F32),
                        pltpu.VMEM((2, PEER_TOPK, tn), F32),
                        pltpu.VMEM((PEER_TOPK, tn), F32),
                        pltpu.VMEM((PEER_SEL, tn), F32),
                        pltpu.VMEM((PEER_SEL, tn), F32)],
        compiler_params=_params(("parallel",)),
        name="peer_topk",
    )(qp, kh, kl)


def _peer_kernel(idx_ref, nxt_ref, xn_ref, g_ref, tab_ref, f_ref, buf0, buf1, sem0, sem1,
                 a_s, c_s, *, tb):
    s = pl.program_id(0)
    n = pl.num_programs(0)
    ncol = PEER_SEL * SUBLANES
    bufs = (buf0, buf1)
    sems = (sem0, sem1)

    def row_copy(e, half, t, r):
        return pltpu.make_async_copy(tab_ref.at[e], bufs[half].at[t, r], sems[half].at[t])

    def issue(ids, row0, half, t, r0, r1):
        for r in range(r0, r1):
            row_copy(ids[row0 + t, r], half, t, r).start(priority=r % 2)

    def drain(half, t):
        for r in range(PEER_SEL):
            row_copy(0, half, t, r).wait()

    @pl.when(s == 0)
    def _():
        def tok(t, carry):
            issue(idx_ref, 0, 0, t, 0, PEER_SEL)
            return carry
        lax.fori_loop(0, tb, tok, 0)

    diag =(lax.broadcasted_iota(I32, (SUBLANES, ncol), 1) & (SUBLANES - 1)) == \
        lax.broadcasted_iota(I32, (SUBLANES, ncol), 0)
    lane = lax.broadcasted_iota(I32, (tb, LANES), 1)
    expand = (lax.broadcasted_iota(I32, (PEER_SEL, ncol), 1) >> 3) == \
        lax.broadcasted_iota(I32, (PEER_SEL, ncol), 0)
    expand = jnp.where(expand, 1.0, 0.0).astype(BF16)

    for half in range(2):
        buf = bufs[half]
        base = half * tb
        nxt_ids, nxt_row0 = (idx_ref, tb) if half == 0 else (nxt_ref, 0)
        for t in range(tb):
            if t % WAIT_GROUP == 0:
                for tt in range(t, t + WAIT_GROUP):
                    drain(half, tt)
            x8 = xn_ref[base + t].astype(BF16)
            u2 = buf[t, :, 0:SUBLANES, :].reshape(ncol, LANES).astype(BF16)
            m = _dot_nt(x8, u2)
            a_s[t:t + 1, :] = jnp.sum(jnp.where(diag, m, 0.0), axis=0, keepdims=True)
            issue(nxt_ids, nxt_row0, 1 - half, t, 0, PEER_SEL // 2)

        cols = []
        for c in range(ncol // LANES):
            x = a_s[:, c * LANES:(c + 1) * LANES]
            for k in (1, 2, 4):
                x = x + jnp.where((lane & k) != 0, pltpu.roll(x, k, axis=1),
                                  pltpu.roll(x, LANES - k, axis=1))
            cols.append(x)
        act = jnp.concatenate(cols, axis=1)
        g_hi, g_lo = _split(g_ref[base:base + tb, :])
        g_rep = _dot(g_hi, expand) + _dot(g_lo, expand)
        c_s[...] = g_rep * jax.nn.gelu(act)

        for t in range(tb):
            c8 = jnp.where(diag, jnp.broadcast_to(c_s[t:t + 1, :], (SUBLANES, ncol)),
                           0.0).astype(BF16)
            v2 = buf[t, :, SUBLANES:2 * SUBLANES, :].reshape(ncol, LANES).astype(BF16)
            f_ref[base + t] = _dot(c8, v2)
            issue(nxt_ids, nxt_row0, 1 - half, t, PEER_SEL // 2, PEER_SEL)

    @pl.when(s == n - 1)
    def _():
        def tok(t, carry):
            drain(0, t)
            return carry
        lax.fori_loop(0, tb, tok, 0)


def _peer(idx, g, xn, table):
    n = xn.shape[0]
    half = PEER_TILE // 2
    tb = PEER_TILE
    steps = n // tb
    kern = functools.partial(_peer_kernel, tb=half)
    smem = lambda fn: pl.BlockSpec((tb, PEER_SEL), fn, memory_space=pltpu.SMEM)
    slab = pl.BlockSpec((tb, SUBLANES, LANES), lambda s: (s, 0, 0))
    out = pl.pallas_call(
        kern,
        grid=(steps,),
        in_specs=[smem(lambda s: (s, 0)),
                  smem(lambda s: (jnp.minimum(s + 1, steps - 1), 0)),
                  slab,
                  pl.BlockSpec((tb, PEER_SEL), lambda s: (s, 0)),
                  pl.BlockSpec(memory_space=pl.ANY)],
        out_specs=slab,
        out_shape=jax.ShapeDtypeStruct((n, SUBLANES, LANES), F32),
        scratch_shapes=[pltpu.VMEM((half, PEER_SEL, 2 * SUBLANES, LANES), F32),
                        pltpu.VMEM((half, PEER_SEL, 2 * SUBLANES, LANES), F32),
                        pltpu.SemaphoreType.DMA((half,)),
                        pltpu.SemaphoreType.DMA((half,)),
                        pltpu.VMEM((half, PEER_SEL * SUBLANES), F32),
                        pltpu.VMEM((half, PEER_SEL * SUBLANES), F32)],
        compiler_params=_params(("arbitrary",)),
        name="peer_gather",
    )(idx, idx, xn.reshape(n, SUBLANES, LANES), g, table)
    return out.reshape(n, D_MODEL)


def _final_kernel(h_ref, f_ref, p_ref, gp_ref, wg_ref, wp_ref, gf_ref, y_ref):
    h = h_ref[...] + f_ref[...]
    gate = jax.nn.sigmoid(_dot(_rms(h, gp_ref[...]).astype(BF16), wg_ref[...]))
    h = h + gate * _dot(p_ref[...].astype(BF16), wp_ref[...])
    y_ref[...] = _rms(h, gf_ref[...])


def _final(h, f, p, g_ple, wg_bf, wp_bf, g_final):
    n = h.shape[0]
    tm = min(TOKEN_TILE, n)
    row = lambda d: pl.BlockSpec((tm, d), lambda i: (i, 0))
    return pl.pallas_call(
        _final_kernel,
        grid=(n // tm,),
        in_specs=[row(D_MODEL), row(D_MODEL), row(p.shape[1]), _const_spec((1, D_MODEL)),
                  _const_spec(wg_bf.shape), _const_spec(wp_bf.shape), _const_spec((1, D_MODEL))],
        out_specs=row(D_MODEL),
        out_shape=jax.ShapeDtypeStruct((n, D_MODEL), F32),
        compiler_params=_params(("parallel",)),
        name="ple_final",
    )(h, f, p, g_ple.reshape(1, D_MODEL), wg_bf, wp_bf, g_final.reshape(1, D_MODEL))


def _trunk(x, p, conv_state, lru_state, win_k, win_v, w):
    bsz, t, _ = x.shape
    n = bsz * t
    xb, gate, q, k, v, kb, vb = _inproj(x.reshape(n, D_MODEL), w["g_mix"], w["w_in"])
    seq = lambda a: a.reshape(bsz, t, WIDTH)
    xb3 = seq(xb)
    if conv_state is None:
        conv_state = jnp.zeros((bsz, 3, WIDTH), F32)
        lru_state = jnp.zeros((bsz, WIDTH), F32)
    yl, h_last = _lru(xb3, seq(gate), conv_state, lru_state, w["conv_w"], w["conv_b"],
                      w["w_ra"], w["b_ra"], w["w_ri"], w["b_ri"], w["lam"], w["g_out_lru"])
    if win_k is None:
        oa = _attn_prompt(seq(q), seq(kb), seq(vb), w["g_out_attn"])
        keep = min(MAX_WINDOW, t)
        new_k, new_v = seq(k)[:, t - keep:], seq(v)[:, t - keep:]
    else:
        past = win_k.shape[1]
        oa = _attn_sample(seq(q), win_k.reshape(bsz, past, WIDTH), win_v.reshape(bsz, past, WIDTH),
                          seq(kb), seq(vb), w["g_out_attn"])
        new_k, new_v = seq(k), seq(v)
    h1, xn2, qp = _outproj(yl.reshape(n, WIDTH), oa.reshape(n, WIDTH), x.reshape(n, D_MODEL),
                           w["w_out"], w["g_ffn"], w["w_peer_q"])
    idx, g = _topk(qp, w["sub_keys"])
    f = _peer(idx, g, xn2, w["table"])
    y = _final(h1, f, p.reshape(n, p.shape[-1]), w["g_ple"], w["w_ple_gate"], w["w_ple_proj"],
               w["g_final"])
    heads = lambda a: a.reshape(1, bsz, a.shape[1], ATTN_HEADS, HEAD_DIM)
    return (y.reshape(bsz, t, D_MODEL), xb3[:, t - 3:][None], h_last.reshape(1, bsz, WIDTH),
            heads(new_k), heads(new_v))


def kernel(x_prompt, x_sample, state_conv, state_lru, cache_win_k, cache_win_v, p_prompt, p_sample,
           g_mix, w_in, conv_w, conv_b, w_ra, b_ra, w_ri, b_ri, lru_lambda, g_out_lru, g_out_attn,
           w_out, g_ffn, w_peer_q, peer_sub_keys, peer_u, peer_v, g_ple, w_ple_gate, w_ple_proj,
           g_final):
    w = {
        "g_mix": g_mix[0].reshape(1, D_MODEL), "w_in": w_in[0].astype(BF16),
        "conv_w": conv_w[0], "conv_b": conv_b[0], "w_ra": w_ra[0], "b_ra": b_ra[0],
        "w_ri": w_ri[0], "b_ri": b_ri[0], "lam": lru_lambda[0],
        "g_out_lru": g_out_lru[0], "g_out_attn": g_out_attn[0],
        "w_out": w_out[0].astype(BF16), "g_ffn": g_ffn[0], "w_peer_q": w_peer_q[0].astype(BF16),
        "sub_keys": peer_sub_keys[0],
        "table": jnp.concatenate([peer_u[0].reshape(-1, SUBLANES, LANES),
                                  peer_v[0].reshape(-1, SUBLANES, LANES)], axis=1),
        "g_ple": g_ple[0], "w_ple_gate": w_ple_gate[0].astype(BF16),
        "w_ple_proj": w_ple_proj[0].astype(BF16), "g_final": g_final,
    }
    yp, pc, plru, pk, pv = _trunk(x_prompt, p_prompt[0], None, None, None, None, w)
    ys, sc, slru, sk, sv = _trunk(x_sample, p_sample[0], state_conv[0], state_lru[0],
                                  cache_win_k[0], cache_win_v[0], w)
    return (yp, ys, pc, plru, pk, pv, sc, slru, sk, sv)
```

```python
import functools

import jax
import jax.numpy as jnp
from jax import lax
from jax.experimental import pallas as pl
from jax.experimental.pallas import tpu as pltpu

F32 = jnp.float32
BF16 = jnp.bfloat16
I32 = jnp.int32

EPS = 1e-6
D_MODEL = 1024
HEAD_DIM = 64
ATTN_HEADS = 8
WIDTH = 512
LRU_C = 8.0
DILATED_GROUPS = ((128, 1), (512, 4), (2048, 16))
MAX_WINDOW = 2048
PEER_HEADS = 8
PEER_KEYS = 128
PEER_HALF = 128
PEER_TOPK = 16
PEER_SEL = PEER_HEADS * PEER_TOPK
NEG = -1e30

LANES = 128
SUBLANES = 8
VMEM_LIMIT = 48 * 1024 * 1024

TOKEN_TILE = 512
LRU_TILE = 256
ATTN_TILE = 256
TOPK_TILE = 128
PEER_TILE = 16


def _params(sem):
    return pltpu.CompilerParams(dimension_semantics=sem, vmem_limit_bytes=VMEM_LIMIT)


def _rms(x, g):
    return x * lax.rsqrt(jnp.mean(x * x, axis=-1, keepdims=True) + EPS) * g


def _dot(a, b):
    return jnp.dot(a, b, preferred_element_type=F32)


def _dot_nt(a, b):
    return lax.dot_general(a, b, (((1,), (1,)), ((), ())), preferred_element_type=F32)


def _split(a):
    hi = a.astype(BF16)
    lo = (a - hi.astype(F32)).astype(BF16)
    return hi, lo


def _const_spec(shape):
    nd = len(shape)
    return pl.BlockSpec(shape, lambda *_: (0,) * nd)


def _inproj_kernel(x_ref, g_ref, w_ref, xb_ref, gate_ref, q_ref, k_ref, v_ref, kb_ref, vb_ref):
    xn = _rms(x_ref[...], g_ref[...]).astype(BF16)

    def proj(c):
        return _dot(xn, w_ref[:, c * WIDTH:(c + 1) * WIDTH])

    xb_ref[...] = proj(0)
    gate_ref[...] = proj(1)
    q_ref[...] = (proj(2) * (HEAD_DIM ** -0.5)).astype(BF16)
    k = proj(3)
    k_ref[...] = k
    kb_ref[...] = k.astype(BF16)
    v = proj(4)
    v_ref[...] = v
    vb_ref[...] = v.astype(BF16)


def _inproj(x, g, w_bf):
    n = x.shape[0]
    tm = min(TOKEN_TILE, n)
    row = lambda d: pl.BlockSpec((tm, d), lambda i: (i, 0))
    f = jax.ShapeDtypeStruct((n, WIDTH), F32)
    b = jax.ShapeDtypeStruct((n, WIDTH), BF16)
    return pl.pallas_call(
        _inproj_kernel,
        grid=(n // tm,),
        in_specs=[row(D_MODEL), _const_spec((1, D_MODEL)), _const_spec(w_bf.shape)],
        out_specs=[row(WIDTH)] * 7,
        out_shape=[f, f, b, f, f, b, b],
        compiler_params=_params(("parallel",)),
        name="inproj",
    )(x, g, w_bf)


def _lru_kernel(xb_ref, gate_ref, cs_ref, h0_ref, cw_ref, cb_ref, wah_ref, wal_ref, ba_ref,
                wih_ref, wil_ref, bi_ref, lam_ref, g_ref, yl_ref, hl_ref,
                xpad, hc, a_s, b_s, h_s, *, tb):
    j = pl.program_id(1)

    @pl.when(j == 0)
    def _():
        xpad[0:SUBLANES, :] = jnp.zeros((SUBLANES, WIDTH), F32)
        xpad[SUBLANES - 3:SUBLANES, :] = cs_ref[...]
        hc[...] = jnp.broadcast_to(h0_ref[...], (SUBLANES, WIDTH))

    x = xb_ref[...]
    xpad[SUBLANES:SUBLANES + tb, :] = x
    w = cw_ref[...]
    xc = cb_ref[...] + xpad[SUBLANES - 3:SUBLANES - 3 + tb, :] * w[0:1]
    xc = xc + xpad[SUBLANES - 2:SUBLANES - 2 + tb, :] * w[1:2]
    xc = xc + xpad[SUBLANES - 1:SUBLANES - 1 + tb, :] * w[2:3]
    xc = xc + x * w[3:4]
    xpad[0:SUBLANES, :] = xpad[tb:tb + SUBLANES, :]

    x_hi, x_lo = _split(xc)

    def gate_dot(wh_ref, wl_ref):
        wh = wh_ref[...]
        return _dot(x_hi, wh) + _dot(x_lo, wh) + _dot(x_hi, wl_ref[...])

    r = jax.nn.sigmoid(gate_dot(wah_ref, wal_ref) + ba_ref[...])
    i = jax.nn.sigmoid(gate_dot(wih_ref, wil_ref) + bi_ref[...])
    nl = -lam_ref[...]
    softplus = jnp.maximum(nl, 0.0) + jnp.log1p(jnp.exp(-jnp.abs(nl)))
    log_a = -LRU_C * r * softplus
    a = jnp.exp(log_a)
    a_s[...] = a
    b_s[...] = jnp.sqrt(jnp.tanh(-log_a) * (a * a + 1.0)) * (i * xc)

    rows = lax.broadcasted_iota(I32, (SUBLANES, WIDTH), 0)

    def group(gi, hb):
        off = pl.multiple_of(gi * SUBLANES, SUBLANES)
        av = a_s[pl.ds(off, SUBLANES), :]
        bv = b_s[pl.ds(off, SUBLANES), :]
        for d in (1, 2, 4):
            a_sh = pltpu.roll(av, d, axis=0)
            b_sh = pltpu.roll(bv, d, axis=0)
            m = rows >= d
            bv = jnp.where(m, av * b_sh + bv, bv)
            av = jnp.where(m, av * a_sh, av)
        h = av * hb + bv
        h_s[pl.ds(off, SUBLANES), :] = h
        return jnp.broadcast_to(h[SUBLANES - 1:SUBLANES, :], (SUBLANES, WIDTH))

    hb = lax.fori_loop(0, tb // SUBLANES, group, hc[...])
    hc[...] = hb

    y = h_s[...] * jax.nn.gelu(gate_ref[...])
    yl_ref[...] = _rms(y, g_ref[...]).astype(BF16)

    @pl.when(j == pl.num_programs(1) - 1)
    def _():
        hl_ref[...] = hb[0:1, :]


def _lru(xb, gate, conv_state, h0, conv_w, conv_b, w_ra, b_ra, w_ri, b_ri, lam, g_out):
    bsz, t, _ = xb.shape
    tb = min(LRU_TILE, t)
    wah, wal = _split(jax.scipy.linalg.block_diag(*w_ra))
    wih, wil = _split(jax.scipy.linalg.block_diag(*w_ri))
    seq = pl.BlockSpec((None, tb, WIDTH), lambda b, j: (b, j, 0))
    vec = _const_spec((1, WIDTH))
    sq = _const_spec((WIDTH, WIDTH))
    kern = functools.partial(_lru_kernel, tb=tb)
    return pl.pallas_call(
        kern,
        grid=(bsz, t // tb),
        in_specs=[seq, seq,
                  pl.BlockSpec((None, 3, WIDTH), lambda b, j: (b, 0, 0)),
                  pl.BlockSpec((None, 1, WIDTH), lambda b, j: (b, 0, 0)),
                  _const_spec((4, WIDTH)), vec, sq, sq, vec, sq, sq, vec, vec, vec],
        out_specs=[seq, pl.BlockSpec((None, 1, WIDTH), lambda b, j: (b, 0, 0))],
        out_shape=[jax.ShapeDtypeStruct((bsz, t, WIDTH), BF16),
                   jax.ShapeDtypeStruct((bsz, 1, WIDTH), F32)],
        scratch_shapes=[pltpu.VMEM((tb + SUBLANES, WIDTH), F32),
                        pltpu.VMEM((SUBLANES, WIDTH), F32),
                        pltpu.VMEM((tb, WIDTH), F32),
                        pltpu.VMEM((tb, WIDTH), F32),
                        pltpu.VMEM((tb, WIDTH), F32)],
        compiler_params=_params(("parallel", "arbitrary")),
        name="rg_lru",
    )(xb, gate, conv_state, h0.reshape(bsz, 1, WIDTH), conv_w, conv_b.reshape(1, WIDTH),
      wah, wal, b_ra.reshape(1, WIDTH), wih, wil, b_ri.reshape(1, WIDTH),
      lam.reshape(1, WIDTH), g_out.reshape(1, WIDTH))


LOCAL_WINDOW = 512
FAR_WINDOW, FAR_DIL = DILATED_GROUPS[-1]
assert all(w <= LOCAL_WINDOW for w, _ in DILATED_GROUPS[:-1]) and LOCAL_WINDOW % FAR_DIL == 0


def _tap_count(d, limit=MAX_WINDOW):
    one = jnp.ones(d.shape, F32)
    zero = jnp.zeros(d.shape, F32)
    cnt = zero
    for window, dil in DILATED_GROUPS:
        hit = jnp.where(d <= min(window, limit), one, zero)
        if dil > 1:
            hit = jnp.where((d & (dil - 1)) == 0, hit, zero)
        cnt = cnt + hit
    return jnp.where(d >= 0, cnt, zero)


def _pair_masks(rows):
    lane = lax.broadcasted_iota(I32, (rows, LANES), 1)
    return lane < HEAD_DIM


def _attn_block(q_ref, k_ref, v_ref, cnt, m_s, l_s, acc_s, rows):
    lo = _pair_masks(rows)
    valid = cnt > 0.0
    zb = jnp.zeros((rows, LANES), BF16)
    for hp in range(ATTN_HEADS // 2):
        sl = slice(hp * LANES, (hp + 1) * LANES)
        qp = q_ref[:, sl]
        kp = k_ref[:, sl]
        vp = v_ref[:, sl]
        qs = (jnp.where(lo, qp, zb), jnp.where(lo, zb, qp))
        ps, alphas = [], []
        for hh in range(2):
            h = 2 * hp + hh
            s = jnp.where(valid, _dot_nt(qs[hh], kp), NEG)
            m_old = m_s[h][:, 0:1]
            m_new = jnp.maximum(m_old, jnp.max(s, axis=-1, keepdims=True))
            p = jnp.exp(s - m_new) * cnt
            alpha = jnp.exp(m_old - m_new)
            l_new = alpha * l_s[h][:, 0:1] + jnp.sum(p, axis=-1, keepdims=True)
            m_s[h] = jnp.broadcast_to(m_new, (rows, LANES))
            l_s[h] = jnp.broadcast_to(l_new, (rows, LANES))
            ps.append(p.astype(BF16))
            alphas.append(alpha)
        pcat = jnp.concatenate(ps, axis=1)
        vbd = jnp.concatenate([jnp.where(lo, vp, zb), jnp.where(lo, zb, vp)], axis=0)
        alpha_pair = jnp.where(lo, alphas[0], alphas[1])
        acc_s[:, sl] = alpha_pair * acc_s[:, sl] + _dot(pcat, vbd)


def _attn_init(m_s, l_s, acc_s):
    m_s[...] = jnp.full(m_s.shape, NEG, F32)
    l_s[...] = jnp.zeros(l_s.shape, F32)
    acc_s[...] = jnp.zeros(acc_s.shape, F32)


def _per_head(ref, lo):
    return jnp.concatenate([jnp.where(lo, ref[2 * hp][:, 0:1], ref[2 * hp + 1][:, 0:1])
                            for hp in range(ATTN_HEADS // 2)], axis=1)


def _attn_local_kernel(q_ref, k_ref, v_ref, acc_ref, m_ref, l_ref, m_s, l_s, acc_s, *, tq, nkb):
    i = pl.program_id(1)
    j = pl.program_id(2)

    @pl.when(j == 0)
    def _():
        _attn_init(m_s, l_s, acc_s)

    kb = i - (nkb - 1) + j

    @pl.when(kb >= 0)
    def _():
        r = lax.broadcasted_iota(I32, (tq, tq), 0)
        c = lax.broadcasted_iota(I32, (tq, tq), 1)
        cnt = _tap_count((i - kb) * tq + r - c, LOCAL_WINDOW)
        _attn_block(q_ref, k_ref, v_ref, cnt, m_s, l_s, acc_s, tq)

    @pl.when(j == nkb - 1)
    def _():
        lo = _pair_masks(tq)
        acc_ref[...] = acc_s[...]
        m_ref[...] = _per_head(m_s, lo)
        l_ref[...] = _per_head(l_s, lo)


def _attn_far_kernel(q_ref, k_ref, v_ref, acc_ref, m_ref, l_ref, g_ref, o_ref, m_s, l_s, acc_s,
                     *, td):
    i = pl.program_id(2)
    j = pl.program_id(3)

    @pl.when(j == 0)
    def _():
        _attn_init(m_s, l_s, acc_s)

    kb = i - 1 + j

    @pl.when(kb >= 0)
    def _():
        r = lax.broadcasted_iota(I32, (td, td), 0)
        c = lax.broadcasted_iota(I32, (td, td), 1)
        d = (i - kb) * td + r - c
        far = jnp.where(d > LOCAL_WINDOW // FAR_DIL, 1.0, 0.0)
        cnt = jnp.where(d <= FAR_WINDOW // FAR_DIL, far, 0.0)
        _attn_block(q_ref, k_ref, v_ref, cnt, m_s, l_s, acc_s, td)

    @pl.when(j == 1)
    def _():
        lo = _pair_masks(td)
        m_far = _per_head(m_s, lo)
        m_loc = m_ref[...]
        m_tot = jnp.maximum(m_loc, m_far)
        w_loc = jnp.exp(m_loc - m_tot)
        w_far = jnp.exp(m_far - m_tot)
        num = acc_ref[...] * w_loc + acc_s[...] * w_far
        den = l_ref[...] * w_loc + _per_head(l_s, lo) * w_far
        o_ref[...] = _rms(num / den, g_ref[...]).astype(BF16)


def _attn_prompt(q, k, v, g):
    bsz, t, _ = q.shape
    tq = min(ATTN_TILE, t)
    nkb = LOCAL_WINDOW // tq + 1
    qspec = pl.BlockSpec((None, tq, WIDTH), lambda b, i, j: (b, i, 0))
    kspec = pl.BlockSpec((None, tq, WIDTH), lambda b, i, j: (b, jnp.maximum(i - (nkb - 1) + j, 0), 0))
    stat = jax.ShapeDtypeStruct((bsz, t, WIDTH), F32)
    acc, m, l = pl.pallas_call(
        functools.partial(_attn_local_kernel, tq=tq, nkb=nkb),
        grid=(bsz, t // tq, nkb),
        in_specs=[qspec, kspec, kspec],
        out_specs=[qspec] * 3,
        out_shape=[stat] * 3,
        scratch_shapes=[pltpu.VMEM((ATTN_HEADS, tq, LANES), F32),
                        pltpu.VMEM((ATTN_HEADS, tq, LANES), F32),
                        pltpu.VMEM((tq, WIDTH), F32)],
        compiler_params=_params(("parallel", "parallel", "arbitrary")),
        name="attn_local",
    )(q, k, v)

    tr = t // FAR_DIL
    td = min(FAR_WINDOW // FAR_DIL, tr)
    view = lambda a: a.reshape(bsz, tr, FAR_DIL * WIDTH)
    dq = pl.BlockSpec((None, td, WIDTH), lambda b, r, i, j: (b, i, r))
    dk = pl.BlockSpec((None, td, WIDTH), lambda b, r, i, j: (b, jnp.maximum(i - 1 + j, 0), r))
    out = pl.pallas_call(
        functools.partial(_attn_far_kernel, td=td),
        grid=(bsz, FAR_DIL, tr // td, 2),
        in_specs=[dq, dk, dk, dq, dq, dq, _const_spec((1, WIDTH))],
        out_specs=dq,
        out_shape=jax.ShapeDtypeStruct((bsz, tr, FAR_DIL * WIDTH), BF16),
        scratch_shapes=[pltpu.VMEM((ATTN_HEADS, td, LANES), F32),
                        pltpu.VMEM((ATTN_HEADS, td, LANES), F32),
                        pltpu.VMEM((td, WIDTH), F32)],
        compiler_params=_params(("parallel", "parallel", "parallel", "arbitrary")),
        name="attn_far",
    )(view(q), view(k), view(v), view(acc), view(m), view(l), g.reshape(1, WIDTH))
    return out.reshape(bsz, t, WIDTH)


def _attn_sample_kernel(q_ref, kc_ref, vc_ref, kn_ref, vn_ref, g_ref, o_ref, *, t, past):
    lo_q = _pair_masks(t)
    lo_c = _pair_masks(past)
    rq = lax.broadcasted_iota(I32, (t, past), 0)
    cp = lax.broadcasted_iota(I32, (t, past), 1)
    cnt_c = _tap_count(past + rq - cp)
    r2 = lax.broadcasted_iota(I32, (t, t), 0)
    c2 = lax.broadcasted_iota(I32, (t, t), 1)
    cnt_n = _tap_count(r2 - c2)
    val_c = cnt_c > 0.0
    val_n = cnt_n > 0.0
    zq = jnp.zeros((t, LANES), BF16)
    zc = jnp.zeros((past, LANES), BF16)
    parts = []
    for hp in range(ATTN_HEADS // 2):
        sl = slice(hp * LANES, (hp + 1) * LANES)
        qp = q_ref[:, sl]
        kc = kc_ref[:, sl].astype(BF16)
        vc = vc_ref[:, sl].astype(BF16)
        kn = kn_ref[:, sl]
        vn = vn_ref[:, sl]
        qs = (jnp.where(lo_q, qp, zq), jnp.where(lo_q, zq, qp))
        pcs, pns, ls = [], [], []
        for hh in range(2):
            s_c = jnp.where(val_c, _dot_nt(qs[hh], kc), NEG)
            s_n = jnp.where(val_n, _dot_nt(qs[hh], kn), NEG)
            m = jnp.maximum(jnp.max(s_c, axis=-1, keepdims=True), jnp.max(s_n, axis=-1, keepdims=True))
            p_c = jnp.exp(s_c - m) * cnt_c
            p_n = jnp.exp(s_n - m) * cnt_n
            ls.append(jnp.sum(p_c, axis=-1, keepdims=True) + jnp.sum(p_n, axis=-1, keepdims=True))
            pcs.append(p_c.astype(BF16))
            pns.append(p_n.astype(BF16))
        vbd_c = jnp.concatenate([jnp.where(lo_c, vc, zc), jnp.where(lo_c, zc, vc)], axis=0)
        vbd_n = jnp.concatenate([jnp.where(lo_q, vn, zq), jnp.where(lo_q, zq, vn)], axis=0)
        pv = _dot(jnp.concatenate(pcs, axis=1), vbd_c) + _dot(jnp.concatenate(pns, axis=1), vbd_n)
        parts.append(pv / jnp.where(lo_q, ls[0], ls[1]))
    o = jnp.concatenate(parts, axis=1)
    o_ref[...] = _rms(o, g_ref[...]).astype(BF16)


def _attn_sample(q, k_cache, v_cache, k_new, v_new, g):
    bsz, t, _ = q.shape
    past = k_cache.shape[1]
    new = pl.BlockSpec((None, t, WIDTH), lambda b: (b, 0, 0))
    old = pl.BlockSpec((None, past, WIDTH), lambda b: (b, 0, 0))
    kern = functools.partial(_attn_sample_kernel, t=t, past=past)
    return pl.pallas_call(
        kern,
        grid=(bsz,),
        in_specs=[new, old, old, new, new, _const_spec((1, WIDTH))],
        out_specs=new,
        out_shape=jax.ShapeDtypeStruct((bsz, t, WIDTH), BF16),
        compiler_params=_params(("parallel",)),
        name="attn_sample",
    )(q, k_cache, v_cache, k_new, v_new, g.reshape(1, WIDTH))


def _outproj_kernel(yl_ref, oa_ref, x_ref, wt_ref, wb_ref, g_ref, wq_ref, h_ref, xn_ref, qp_ref):
    h = x_ref[...] + (_dot(yl_ref[...], wt_ref[...]) + _dot(oa_ref[...], wb_ref[...]))
    h_ref[...] = h
    xn = _rms(h, g_ref[...])
    xn_ref[...] = xn
    qp_ref[...] = _dot(xn.astype(BF16), wq_ref[...])


def _outproj(yl, oa, x, w_out_bf, g_ffn, wq_bf):
    n = x.shape[0]
    tm = min(TOKEN_TILE, n)
    nq = wq_bf.shape[1]
    row = lambda d: pl.BlockSpec((tm, d), lambda i: (i, 0))
    return pl.pallas_call(
        _outproj_kernel,
        grid=(n // tm,),
        in_specs=[row(WIDTH), row(WIDTH), row(D_MODEL),
                  _const_spec((WIDTH, D_MODEL)), _const_spec((WIDTH, D_MODEL)),
                  _const_spec((1, D_MODEL)), _const_spec(wq_bf.shape)],
        out_specs=[row(D_MODEL), row(D_MODEL), row(nq)],
        out_shape=[jax.ShapeDtypeStruct((n, D_MODEL), F32),
                   jax.ShapeDtypeStruct((n, D_MODEL), F32),
                   jax.ShapeDtypeStruct((n, nq), F32)],
        compiler_params=_params(("parallel",)),
        name="outproj",
    )(yl, oa, x, w_out_bf[:WIDTH], w_out_bf[WIDTH:], g_ffn.reshape(1, D_MODEL), wq_bf)


def _first_max(vals, tags, extra=None):
    pay = [tags] + ([extra] if extra is not None else [])
    while len(vals) > 1:
        nv, npay = [], [[] for _ in pay]
        for a in range(0, len(vals) - 1, 2):
            take_b = vals[a + 1] > vals[a]
            nv.append(jnp.maximum(vals[a], vals[a + 1]))
            for k, p in enumerate(pay):
                npay[k].append(jnp.where(take_b, p[a + 1], p[a]))
        if len(vals) % 2:
            nv.append(vals[-1])
            for k, p in enumerate(pay):
                npay[k].append(p[-1])
        vals, pay = nv, npay
    v = vals[0]
    pay = [p[0] for p in pay]
    for d in (4, 2, 1):
        vr = pltpu.roll(v, d, axis=0)
        pr = [pltpu.roll(p, d, axis=0) for p in pay]
        better = (vr > v) | ((vr == v) & (pr[0] < pay[0]))
        v = jnp.where(better, vr, v)
        pay = [jnp.where(better, a, b) for a, b in zip(pr, pay)]
    return (v, *pay)


def _topk_kernel(qp_ref, kh_ref, kl_ref, idx_ref, g_ref, top_s, top_i, best_s, out_g, out_i, *, tn):
    sub = lax.broadcasted_iota(I32, (SUBLANES, tn), 0)
    subf = sub.astype(F32)
    nslab = PEER_KEYS // SUBLANES
    key_tags = [subf + float(SUBLANES * k) for k in range(nslab)]
    ninf = jnp.full((SUBLANES, tn), -jnp.inf, F32)
    for h in range(PEER_HEADS):
        for c in range(2):
            hc = 2 * h + c
            q_hi, q_lo = _split(qp_ref[:, hc * PEER_HALF:(hc + 1) * PEER_HALF])
            kh = kh_ref[hc]
            s = _dot_nt(kh, q_hi) + _dot_nt(kh, q_lo) + _dot_nt(kl_ref[hc], q_hi)
            slabs = [s[SUBLANES * k:SUBLANES * (k + 1), :] for k in range(nslab)]
            for r in range(PEER_TOPK):
                m, am = _first_max(slabs, key_tags)
                slabs = [jnp.where(t == am, ninf, x) for x, t in zip(slabs, key_tags)]
                top_s[c, r:r + 1, :] = m[0:1, :]
                top_i[c, r:r + 1, :] = am[0:1, :]
        s0_lo, s0_hi = top_s[0, 0:SUBLANES, :], top_s[0, SUBLANES:2 * SUBLANES, :]
        i0_lo, i0_hi = top_i[0, 0:SUBLANES, :], top_i[0, SUBLANES:2 * SUBLANES, :]
        s1_lo, s1_hi = top_s[1, 0:SUBLANES, :], top_s[1, SUBLANES:2 * SUBLANES, :]
        i1_lo, i1_hi = top_i[1, 0:SUBLANES, :], top_i[1, SUBLANES:2 * SUBLANES, :]
        bcast = lambda x, a: jnp.broadcast_to(x[a:a + 1, :], (SUBLANES, tn))
        cs = [bcast(s0_lo, 0) + s1_lo, bcast(s0_lo, 0) + s1_hi]
        ci = [bcast(i0_lo, 0) * PEER_KEYS + i1_lo, bcast(i0_lo, 0) * PEER_KEYS + i1_hi]
        pos = [subf, subf + float(SUBLANES)]
        for a in range(1, SUBLANES):
            nb = PEER_TOPK // (a + 1)
            cs.append(jnp.where(sub < nb, bcast(s0_lo, a) + s1_lo, ninf))
            ci.append(bcast(i0_lo, a) * PEER_KEYS + i1_lo)
            pos.append(subf + float(a * PEER_TOPK))
        cs.append(s0_hi + bcast(s1_lo, 0))
        ci.append(i0_hi * PEER_KEYS + bcast(i1_lo, 0))
        pos.append((subf + float(SUBLANES)) * PEER_TOPK)
        for r in range(PEER_TOPK):
            m, p, e = _first_max(cs, pos, ci)
            cs = [jnp.where(t == p, ninf, x) for x, t in zip(cs, pos)]
            row = h * PEER_TOPK + r
            out_i[row:row + 1, :] = e[0:1, :]
            best_s[r:r + 1, :] = m[0:1, :]
        bs = best_s[...]
        ex = jnp.exp(bs - bs[0:1, :])
        out_g[h * PEER_TOPK:(h + 1) * PEER_TOPK, :] = ex / jnp.sum(ex, axis=0, keepdims=True)
    g_ref[...] = out_g[...].T
    idx_ref[...] = out_i[...].T.astype(I32)


def _topk(qp, sub_keys):
    n = qp.shape[0]
    tn = min(TOPK_TILE, n)
    kh, kl = _split(sub_keys.reshape(PEER_HEADS * 2, PEER_KEYS, PEER_HALF))
    kern = functools.partial(_topk_kernel, tn=tn)
    return pl.pallas_call(
        kern,
        grid=(n // tn,),
        in_specs=[pl.BlockSpec((tn, qp.shape[1]), lambda i: (i, 0)),
                  _const_spec(kh.shape), _const_spec(kl.shape)],
        out_specs=[pl.BlockSpec((tn, PEER_SEL), lambda i: (i, 0))] * 2,
        out_shape=[jax.ShapeDtypeStruct((n, PEER_SEL), I32),
                   jax.ShapeDtypeStruct((n, PEER_SEL), F32)],
        scratch_shapes=[pltpu.VMEM((2, PEER_TOPK, tn), F32),
                        pltpu.VMEM((2, PEER_TOPK, tn), F32),
                        pltpu.VMEM((PEER_TOPK, tn), F32),
                        pltpu.VMEM((PEER_SEL, tn), F32),
                        pltpu.VMEM((PEER_SEL, tn), F32)],
        compiler_params=_params(("parallel",)),
        name="peer_topk",
    )(qp, kh, kl)


def _peer_kernel(idx_ref, nxt_ref, xn_ref, g_ref, tab_ref, f_ref, buf, sem, a_s, c_s, *, tb):
    s = pl.program_id(0)
    n = pl.num_programs(0)
    slot = lax.rem(s, 2)
    other = 1 - slot
    ncol = PEER_SEL * SUBLANES

    def row_copy(e, slot_, t, r):
        return pltpu.make_async_copy(tab_ref.at[e], buf.at[slot_, t, r], sem.at[slot_, t])

    def issue(ids, slot_, t, r0, r1):
        for r in range(r0, r1):
            row_copy(ids[t, r], slot_, t, r).start(priority=r % 2)

    def drain(slot_, t):
        for r in range(PEER_SEL):
            row_copy(0, slot_, t, r).wait()

    @pl.when(s == 0)
    def _():
        def tok(t, carry):
            issue(idx_ref, 0, t, 0, PEER_SEL)
            return carry
        lax.fori_loop(0, tb, tok, 0)

    diag = (lax.broadcasted_iota(I32, (SUBLANES, ncol), 1) & (SUBLANES - 1)) == \
        lax.broadcasted_iota(I32, (SUBLANES, ncol), 0)

    for t in range(tb):
        drain(slot, t)
        x8 = xn_ref[t].astype(BF16)
        u2 = buf[slot, t, :, 0:SUBLANES, :].reshape(ncol, LANES).astype(BF16)
        m = _dot_nt(x8, u2)
        a_s[t:t + 1, :] = jnp.sum(jnp.where(diag, m, 0.0), axis=0, keepdims=True)
        issue(nxt_ref, other, t, 0, PEER_SEL // 2)

    lane = lax.broadcasted_iota(I32, (tb, LANES), 1)
    cols = []
    for c in range(ncol // LANES):
        x = a_s[:, c * LANES:(c + 1) * LANES]
        for k in (1, 2, 4):
            x = x + jnp.where((lane & k) != 0, pltpu.roll(x, k, axis=1),
                              pltpu.roll(x, LANES - k, axis=1))
        cols.append(x)
    act = jnp.concatenate(cols, axis=1)
    expand = (lax.broadcasted_iota(I32, (PEER_SEL, ncol), 1) >> 3) == \
        lax.broadcasted_iota(I32, (PEER_SEL, ncol), 0)
    expand = jnp.where(expand, 1.0, 0.0).astype(BF16)
    g_hi, g_lo = _split(g_ref[...])
    g_rep = _dot(g_hi, expand) + _dot(g_lo, expand)
    c_s[...] = g_rep * jax.nn.gelu(act)

    for t in range(tb):
        c8 = jnp.where(diag, jnp.broadcast_to(c_s[t:t + 1, :], (SUBLANES, ncol)), 0.0).astype(BF16)
        v2 = buf[slot, t, :, SUBLANES:2 * SUBLANES, :].reshape(ncol, LANES).astype(BF16)
        f_ref[t] = _dot(c8, v2)
        issue(nxt_ref, other, t, PEER_SEL // 2, PEER_SEL)

    @pl.when(s == n - 1)
    def _():
        def tok(t, carry):
            drain(other, t)
            return carry
        lax.fori_loop(0, tb, tok, 0)


def _peer(idx, g, xn, table):
    n = xn.shape[0]
    tb = PEER_TILE
    steps = n // tb
    kern = functools.partial(_peer_kernel, tb=tb)
    smem = lambda fn: pl.BlockSpec((tb, PEER_SEL), fn, memory_space=pltpu.SMEM)
    slab = pl.BlockSpec((tb, SUBLANES, LANES), lambda s: (s, 0, 0))
    out = pl.pallas_call(
        kern,
        grid=(steps,),
        in_specs=[smem(lambda s: (s, 0)),
                  smem(lambda s: (jnp.minimum(s + 1, steps - 1), 0)),
                  slab,
                  pl.BlockSpec((tb, PEER_SEL), lambda s: (s, 0)),
                  pl.BlockSpec(memory_space=pl.ANY)],
        out_specs=slab,
        out_shape=jax.ShapeDtypeStruct((n, SUBLANES, LANES), F32),
        scratch_shapes=[pltpu.VMEM((2, tb, PEER_SEL, 2 * SUBLANES, LANES), F32),
                        pltpu.SemaphoreType.DMA((2, tb)),
                        pltpu.VMEM((tb, PEER_SEL * SUBLANES), F32),
                        pltpu.VMEM((tb, PEER_SEL * SUBLANES), F32)],
        compiler_params=_params(("arbitrary",)),
        name="peer_gather",
    )(idx, idx, xn.reshape(n, SUBLANES, LANES), g, table)
    return out.reshape(n, D_MODEL)


def _final_kernel(h_ref, f_ref, p_ref, gp_ref, wg_ref, wp_ref, gf_ref, y_ref):
    h = h_ref[...] + f_ref[...]
    gate = jax.nn.sigmoid(_dot(_rms(h, gp_ref[...]).astype(BF16), wg_ref[...]))
    h = h + gate * _dot(p_ref[...].astype(BF16), wp_ref[...])
    y_ref[...] = _rms(h, gf_ref[...])


def _final(h, f, p, g_ple, wg_bf, wp_bf, g_final):
    n = h.shape[0]
    tm = min(TOKEN_TILE, n)
    row = lambda d: pl.BlockSpec((tm, d), lambda i: (i, 0))
    return pl.pallas_call(
        _final_kernel,
        grid=(n // tm,),
        in_specs=[row(D_MODEL), row(D_MODEL), row(p.shape[1]), _const_spec((1, D_MODEL)),
                  _const_spec(wg_bf.shape), _const_spec(wp_bf.shape), _const_spec((1, D_MODEL))],
        out_specs=row(D_MODEL),
        out_shape=jax.ShapeDtypeStruct((n, D_MODEL), F32),
        compiler_params=_params(("parallel",)),
        name="ple_final",
    )(h, f, p, g_ple.reshape(1, D_MODEL), wg_bf, wp_bf, g_final.reshape(1, D_MODEL))


def _trunk(x, p, conv_state, lru_state, win_k, win_v, w):
    bsz, t, _ = x.shape
    n = bsz * t
    xb, gate, q, k, v, kb, vb = _inproj(x.reshape(n, D_MODEL), w["g_mix"], w["w_in"])
    seq = lambda a: a.reshape(bsz, t, WIDTH)
    xb3 = seq(xb)
    if conv_state is None:
        conv_state = jnp.zeros((bsz, 3, WIDTH), F32)
        lru_state = jnp.zeros((bsz, WIDTH), F32)
    yl, h_last = _lru(xb3, seq(gate), conv_state, lru_state, w["conv_w"], w["conv_b"],
                      w["w_ra"], w["b_ra"], w["w_ri"], w["b_ri"], w["lam"], w["g_out_lru"])
    if win_k is None:
        oa = _attn_prompt(seq(q), seq(kb), seq(vb), w["g_out_attn"])
        keep = min(MAX_WINDOW, t)
        new_k, new_v = seq(k)[:, t - keep:], seq(v)[:, t - keep:]
    else:
        past = win_k.shape[1]
        oa = _attn_sample(seq(q), win_k.reshape(bsz, past, WIDTH), win_v.reshape(bsz, past, WIDTH),
                          seq(kb), seq(vb), w["g_out_attn"])
        new_k, new_v = seq(k), seq(v)
    h1, xn2, qp = _outproj(yl.reshape(n, WIDTH), oa.reshape(n, WIDTH), x.reshape(n, D_MODEL),
                           w["w_out"], w["g_ffn"], w["w_peer_q"])
    idx, g = _topk(qp, w["sub_keys"])
    f = _peer(idx, g, xn2, w["table"])
    y = _final(h1, f, p.reshape(n, p.shape[-1]), w["g_ple"], w["w_ple_gate"], w["w_ple_proj"],
               w["g_final"])
    heads = lambda a: a.reshape(1, bsz, a.shape[1], ATTN_HEADS, HEAD_DIM)
    return (y.reshape(bsz, t, D_MODEL), xb3[:, t - 3:][None], h_last.reshape(1, bsz, WIDTH),
            heads(new_k), heads(new_v))


def kernel(x_prompt, x_sample, state_conv, state_lru, cache_win_k, cache_win_v, p_prompt, p_sample,
           g_mix, w_in, conv_w, conv_b, w_ra, b_ra, w_ri, b_ri, lru_lambda, g_out_lru, g_out_attn,
           w_out, g_ffn, w_peer_q, peer_sub_keys, peer_u, peer_v, g_ple, w_ple_gate, w_ple_proj,
           g_final):
    w = {
        "g_mix": g_mix[0].reshape(1, D_MODEL), "w_in": w_in[0].astype(BF16),
        "conv_w": conv_w[0], "conv_b": conv_b[0], "w_ra": w_ra[0], "b_ra": b_ra[0],
        "w_ri": w_ri[0], "b_ri": b_ri[0], "lam": lru_lambda[0],
        "g_out_lru": g_out_lru[0], "g_out_attn": g_out_attn[0],
        "w_out": w_out[0].astype(BF16), "g_ffn": g_ffn[0], "w_peer_q": w_peer_q[0].astype(BF16),
        "sub_keys": peer_sub_keys[0],
        "table": jnp.concatenate([peer_u[0].reshape(-1, SUBLANES, LANES),
                                  peer_v[0].reshape(-1, SUBLANES, LANES)], axis=1),
        "g_ple": g_ple[0], "w_ple_gate": w_ple_gate[0].astype(BF16),
        "w_ple_proj": w_ple_proj[0].astype(BF16), "g_final": g_final,
    }
    yp, pc, plru, pk, pv = _trunk(x_prompt, p_prompt[0], None, None, None, None, w)
    ys, sc, slru, sk, sv = _trunk(x_sample, p_sample[0], state_conv[0], state_lru[0],
                                  cache_win_k[0], cache_win_v[0], w)
    return (yp, ys, pc, plru, pk, pv, sc, slru, sk, sv)
```

```python
import functools

import jax
import jax.numpy as jnp
from jax import lax
from jax.experimental import pallas as pl
from jax.experimental.pallas import tpu as pltpu

F32 = jnp.float32
BF16 = jnp.bfloat16
I32 = jnp.int32

EPS = 1e-6
D_MODEL = 1024
HEAD_DIM = 64
ATTN_HEADS = 8
WIDTH = 512
LRU_C = 8.0
DILATED_GROUPS = ((128, 1), (512, 4), (2048, 16))
MAX_WINDOW = 2048
PEER_HEADS = 8
PEER_KEYS = 128
PEER_HALF = 128
PEER_TOPK = 16
PEER_SEL = PEER_HEADS * PEER_TOPK
NEG = -1e30

LANES = 128
SUBLANES = 8
VMEM_LIMIT = 48 * 1024 * 1024

TOKEN_TILE = 512
LRU_TILE = 256
ATTN_TILE = 256
TOPK_TILE = 128
PEER_TILE = 16
PACK_TILE = 256


def _params(sem):
    return pltpu.CompilerParams(dimension_semantics=sem, vmem_limit_bytes=VMEM_LIMIT)


def _rms(x, g):
    return x * lax.rsqrt(jnp.mean(x * x, axis=-1, keepdims=True) + EPS) * g


def _dot(a, b):
    return jnp.dot(a, b, preferred_element_type=F32)


def _dot_nt(a, b):
    return lax.dot_general(a, b, (((1,), (1,)), ((), ())), preferred_element_type=F32)


def _split(a):
    hi = a.astype(BF16)
    lo = (a - hi.astype(F32)).astype(BF16)
    return hi, lo


def _const_spec(shape):
    nd = len(shape)
    return pl.BlockSpec(shape, lambda *_: (0,) * nd)


def _inproj_kernel(x_ref, g_ref, w_ref, xb_ref, gate_ref, q_ref, k_ref, v_ref, kb_ref, vb_ref):
    xn = _rms(x_ref[...], g_ref[...]).astype(BF16)

    def proj(c):
        return _dot(xn, w_ref[:, c * WIDTH:(c + 1) * WIDTH])

    xb_ref[...] = proj(0)
    gate_ref[...] = proj(1)
    q_ref[...] = (proj(2) * (HEAD_DIM ** -0.5)).astype(BF16)
    k = proj(3)
    k_ref[...] = k
    kb_ref[...] = k.astype(BF16)
    v = proj(4)
    v_ref[...] = v
    vb_ref[...] = v.astype(BF16)


def _inproj(x, g, w_bf):
    n = x.shape[0]
    tm = min(TOKEN_TILE, n)
    row = lambda d: pl.BlockSpec((tm, d), lambda i: (i, 0))
    f = jax.ShapeDtypeStruct((n, WIDTH), F32)
    b = jax.ShapeDtypeStruct((n, WIDTH), BF16)
    return pl.pallas_call(
        _inproj_kernel,
        grid=(n // tm,),
        in_specs=[row(D_MODEL), _const_spec((1, D_MODEL)), _const_spec(w_bf.shape)],
        out_specs=[row(WIDTH)] * 7,
        out_shape=[f, f, b, f, f, b, b],
        compiler_params=_params(("parallel",)),
        name="inproj",
    )(x, g, w_bf)


def _lru_kernel(xb_ref, gate_ref, cs_ref, h0_ref, cw_ref, cb_ref, wah_ref, wal_ref, ba_ref,
                wih_ref, wil_ref, bi_ref, lam_ref, g_ref, yl_ref, hl_ref,
                xpad, hc, a_s, b_s, h_s, *, tb):
    j = pl.program_id(1)

    @pl.when(j == 0)
    def _():
        xpad[0:SUBLANES, :] = jnp.zeros((SUBLANES, WIDTH), F32)
        xpad[SUBLANES - 3:SUBLANES, :] = cs_ref[...]
        hc[...] = jnp.broadcast_to(h0_ref[...], (SUBLANES, WIDTH))

    x = xb_ref[...]
    xpad[SUBLANES:SUBLANES + tb, :] = x
    w = cw_ref[...]
    xc = cb_ref[...] + xpad[SUBLANES - 3:SUBLANES - 3 + tb, :] * w[0:1]
    xc = xc + xpad[SUBLANES - 2:SUBLANES - 2 + tb, :] * w[1:2]
    xc = xc + xpad[SUBLANES - 1:SUBLANES - 1 + tb, :] * w[2:3]
    xc = xc + x * w[3:4]
    xpad[0:SUBLANES, :] = xpad[tb:tb + SUBLANES, :]

    x_hi, x_lo = _split(xc)

    def gate_dot(wh_ref, wl_ref):
        wh = wh_ref[...]
        return _dot(x_hi, wh) + _dot(x_lo, wh) + _dot(x_hi, wl_ref[...])

    r = jax.nn.sigmoid(gate_dot(wah_ref, wal_ref) + ba_ref[...])
    i = jax.nn.sigmoid(gate_dot(wih_ref, wil_ref) + bi_ref[...])
    nl = -lam_ref[...]
    softplus = jnp.maximum(nl, 0.0) + jnp.log1p(jnp.exp(-jnp.abs(nl)))
    log_a = -LRU_C * r * softplus
    a = jnp.exp(log_a)
    a_s[...] = a
    b_s[...] = jnp.sqrt(jnp.tanh(-log_a) * (a * a + 1.0)) * (i * xc)

    rows = lax.broadcasted_iota(I32, (SUBLANES, WIDTH), 0)

    def group(gi, hb):
        off = pl.multiple_of(gi * SUBLANES, SUBLANES)
        av = a_s[pl.ds(off, SUBLANES), :]
        bv = b_s[pl.ds(off, SUBLANES), :]
        for d in (1, 2, 4):
            a_sh = pltpu.roll(av, d, axis=0)
            b_sh = pltpu.roll(bv, d, axis=0)
            m = rows >= d
            bv = jnp.where(m, av * b_sh + bv, bv)
            av = jnp.where(m, av * a_sh, av)
        h = av * hb + bv
        h_s[pl.ds(off, SUBLANES), :] = h
        return jnp.broadcast_to(h[SUBLANES - 1:SUBLANES, :], (SUBLANES, WIDTH))

    hb = lax.fori_loop(0, tb // SUBLANES, group, hc[...])
    hc[...] = hb

    y = h_s[...] * jax.nn.gelu(gate_ref[...])
    yl_ref[...] = _rms(y, g_ref[...]).astype(BF16)

    @pl.when(j == pl.num_programs(1) - 1)
    def _():
        hl_ref[...] = hb[0:1, :]


def _lru(xb, gate, conv_state, h0, conv_w, conv_b, w_ra, b_ra, w_ri, b_ri, lam, g_out):
    bsz, t, _ = xb.shape
    tb = min(LRU_TILE, t)
    wah, wal = _split(jax.scipy.linalg.block_diag(*w_ra))
    wih, wil = _split(jax.scipy.linalg.block_diag(*w_ri))
    seq = pl.BlockSpec((None, tb, WIDTH), lambda b, j: (b, j, 0))
    vec = _const_spec((1, WIDTH))
    sq = _const_spec((WIDTH, WIDTH))
    kern = functools.partial(_lru_kernel, tb=tb)
    return pl.pallas_call(
        kern,
        grid=(bsz, t // tb),
        in_specs=[seq, seq,
                  pl.BlockSpec((None, 3, WIDTH), lambda b, j: (b, 0, 0)),
                  pl.BlockSpec((None, 1, WIDTH), lambda b, j: (b, 0, 0)),
                  _const_spec((4, WIDTH)), vec, sq, sq, vec, sq, sq, vec, vec, vec],
        out_specs=[seq, pl.BlockSpec((None, 1, WIDTH), lambda b, j: (b, 0, 0))],
        out_shape=[jax.ShapeDtypeStruct((bsz, t, WIDTH), BF16),
                   jax.ShapeDtypeStruct((bsz, 1, WIDTH), F32)],
        scratch_shapes=[pltpu.VMEM((tb + SUBLANES, WIDTH), F32),
                        pltpu.VMEM((SUBLANES, WIDTH), F32),
                        pltpu.VMEM((tb, WIDTH), F32),
                        pltpu.VMEM((tb, WIDTH), F32),
                        pltpu.VMEM((tb, WIDTH), F32)],
        compiler_params=_params(("parallel", "arbitrary")),
        name="rg_lru",
    )(xb, gate, conv_state, h0.reshape(bsz, 1, WIDTH), conv_w, conv_b.reshape(1, WIDTH),
      wah, wal, b_ra.reshape(1, WIDTH), wih, wil, b_ri.reshape(1, WIDTH),
      lam.reshape(1, WIDTH), g_out.reshape(1, WIDTH))


LOCAL_WINDOW = 512
FAR_WINDOW, FAR_DIL = DILATED_GROUPS[-1]
assert all(w <= LOCAL_WINDOW for w, _ in DILATED_GROUPS[:-1]) and LOCAL_WINDOW % FAR_DIL == 0


def _tap_count(d, limit=MAX_WINDOW):
    one = jnp.ones(d.shape, F32)
    zero = jnp.zeros(d.shape, F32)
    cnt = zero
    for window, dil in DILATED_GROUPS:
        hit = jnp.where(d <= min(window, limit), one, zero)
        if dil > 1:
            hit = jnp.where((d & (dil - 1)) == 0, hit, zero)
        cnt = cnt + hit
    return jnp.where(d >= 0, cnt, zero)


def _pair_masks(rows):
    lane = lax.broadcasted_iota(I32, (rows, LANES), 1)
    return lane < HEAD_DIM


def _attn_block(q_ref, k_ref, v_ref, cnt, m_s, l_s, acc_s, rows):
    lo = _pair_masks(rows)
    valid = cnt > 0.0
    zb = jnp.zeros((rows, LANES), BF16)
    for hp in range(ATTN_HEADS // 2):
        sl = slice(hp * LANES, (hp + 1) * LANES)
        qp = q_ref[:, sl]
        kp = k_ref[:, sl]
        vp = v_ref[:, sl]
        qs = (jnp.where(lo, qp, zb), jnp.where(lo, zb, qp))
        ps, alphas = [], []
        for hh in range(2):
            h = 2 * hp + hh
            s = jnp.where(valid, _dot_nt(qs[hh], kp), NEG)
            m_old = m_s[h][:, 0:1]
            m_new = jnp.maximum(m_old, jnp.max(s, axis=-1, keepdims=True))
            p = jnp.exp(s - m_new) * cnt
            alpha = jnp.exp(m_old - m_new)
            l_new = alpha * l_s[h][:, 0:1] + jnp.sum(p, axis=-1, keepdims=True)
            m_s[h] = jnp.broadcast_to(m_new, (rows, LANES))
            l_s[h] = jnp.broadcast_to(l_new, (rows, LANES))
            ps.append(p.astype(BF16))
            alphas.append(alpha)
        pcat = jnp.concatenate(ps, axis=1)
        vbd = jnp.concatenate([jnp.where(lo, vp, zb), jnp.where(lo, zb, vp)], axis=0)
        alpha_pair = jnp.where(lo, alphas[0], alphas[1])
        acc_s[:, sl] = alpha_pair * acc_s[:, sl] + _dot(pcat, vbd)


def _attn_init(m_s, l_s, acc_s):
    m_s[...] = jnp.full(m_s.shape, NEG, F32)
    l_s[...] = jnp.zeros(l_s.shape, F32)
    acc_s[...] = jnp.zeros(acc_s.shape, F32)


def _per_head(ref, lo):
    return jnp.concatenate([jnp.where(lo, ref[2 * hp][:, 0:1], ref[2 * hp + 1][:, 0:1])
                            for hp in range(ATTN_HEADS // 2)], axis=1)


def _attn_local_kernel(q_ref, k_ref, v_ref, acc_ref, m_ref, l_ref, m_s, l_s, acc_s, *, tq, nkb):
    i = pl.program_id(1)
    j = pl.program_id(2)

    @pl.when(j == 0)
    def _():
        _attn_init(m_s, l_s, acc_s)

    kb = i - (nkb - 1) + j

    @pl.when(kb >= 0)
    def _():
        r = lax.broadcasted_iota(I32, (tq, tq), 0)
        c = lax.broadcasted_iota(I32, (tq, tq), 1)
        cnt = _tap_count((i - kb) * tq + r - c, LOCAL_WINDOW)
        _attn_block(q_ref, k_ref, v_ref, cnt, m_s, l_s, acc_s, tq)

    @pl.when(j == nkb - 1)
    def _():
        lo = _pair_masks(tq)
        acc_ref[...] = acc_s[...]
        m_ref[...] = _per_head(m_s, lo)
        l_ref[...] = _per_head(l_s, lo)


def _attn_far_kernel(q_ref, k_ref, v_ref, acc_ref, m_ref, l_ref, g_ref, o_ref, m_s, l_s, acc_s,
                     *, td):
    i = pl.program_id(2)
    j = pl.program_id(3)

    @pl.when(j == 0)
    def _():
        _attn_init(m_s, l_s, acc_s)

    kb = i - 1 + j

    @pl.when(kb >= 0)
    def _():
        r = lax.broadcasted_iota(I32, (td, td), 0)
        c = lax.broadcasted_iota(I32, (td, td), 1)
        d = (i - kb) * td + r - c
        far = jnp.where(d > LOCAL_WINDOW // FAR_DIL, 1.0, 0.0)
        cnt = jnp.where(d <= FAR_WINDOW // FAR_DIL, far, 0.0)
        _attn_block(q_ref, k_ref, v_ref, cnt, m_s, l_s, acc_s, td)

    @pl.when(j == 1)
    def _():
        lo = _pair_masks(td)
        m_far = _per_head(m_s, lo)
        m_loc = m_ref[...]
        m_tot = jnp.maximum(m_loc, m_far)
        w_loc = jnp.exp(m_loc - m_tot)
        w_far = jnp.exp(m_far - m_tot)
        num = acc_ref[...] * w_loc + acc_s[...] * w_far
        den = l_ref[...] * w_loc + _per_head(l_s, lo) * w_far
        o_ref[...] = _rms(num / den, g_ref[...]).astype(BF16)


def _attn_prompt(q, k, v, g):
    bsz, t, _ = q.shape
    tq = min(ATTN_TILE, t)
    nkb = LOCAL_WINDOW // tq + 1
    qspec = pl.BlockSpec((None, tq, WIDTH), lambda b, i, j: (b, i, 0))
    kspec = pl.BlockSpec((None, tq, WIDTH), lambda b, i, j: (b, jnp.maximum(i - (nkb - 1) + j, 0), 0))
    stat = jax.ShapeDtypeStruct((bsz, t, WIDTH), F32)
    acc, m, l = pl.pallas_call(
        functools.partial(_attn_local_kernel, tq=tq, nkb=nkb),
        grid=(bsz, t // tq, nkb),
        in_specs=[qspec, kspec, kspec],
        out_specs=[qspec] * 3,
        out_shape=[stat] * 3,
        scratch_shapes=[pltpu.VMEM((ATTN_HEADS, tq, LANES), F32),
                        pltpu.VMEM((ATTN_HEADS, tq, LANES), F32),
                        pltpu.VMEM((tq, WIDTH), F32)],
        compiler_params=_params(("parallel", "parallel", "arbitrary")),
        name="attn_local",
    )(q, k, v)

    tr = t // FAR_DIL
    td = min(FAR_WINDOW // FAR_DIL, tr)
    view = lambda a: a.reshape(bsz, tr, FAR_DIL * WIDTH)
    dq = pl.BlockSpec((None, td, WIDTH), lambda b, r, i, j: (b, i, r))
    dk = pl.BlockSpec((None, td, WIDTH), lambda b, r, i, j: (b, jnp.maximum(i - 1 + j, 0), r))
    out = pl.pallas_call(
        functools.partial(_attn_far_kernel, td=td),
        grid=(bsz, FAR_DIL, tr // td, 2),
        in_specs=[dq, dk, dk, dq, dq, dq, _const_spec((1, WIDTH))],
        out_specs=dq,
        out_shape=jax.ShapeDtypeStruct((bsz, tr, FAR_DIL * WIDTH), BF16),
        scratch_shapes=[pltpu.VMEM((ATTN_HEADS, td, LANES), F32),
                        pltpu.VMEM((ATTN_HEADS, td, LANES), F32),
                        pltpu.VMEM((td, WIDTH), F32)],
        compiler_params=_params(("parallel", "parallel", "parallel", "arbitrary")),
        name="attn_far",
    )(view(q), view(k), view(v), view(acc), view(m), view(l), g.reshape(1, WIDTH))
    return out.reshape(bsz, t, WIDTH)


def _attn_sample_kernel(q_ref, kc_ref, vc_ref, kn_ref, vn_ref, g_ref, o_ref, *, t, past):
    lo_q = _pair_masks(t)
    lo_c = _pair_masks(past)
    rq = lax.broadcasted_iota(I32, (t, past), 0)
    cp = lax.broadcasted_iota(I32, (t, past), 1)
    cnt_c = _tap_count(past + rq - cp)
    r2 = lax.broadcasted_iota(I32, (t, t), 0)
    c2 = lax.broadcasted_iota(I32, (t, t), 1)
    cnt_n = _tap_count(r2 - c2)
    val_c = cnt_c > 0.0
    val_n = cnt_n > 0.0
    zq = jnp.zeros((t, LANES), BF16)
    zc = jnp.zeros((past, LANES), BF16)
    parts = []
    for hp in range(ATTN_HEADS // 2):
        sl = slice(hp * LANES, (hp + 1) * LANES)
        qp = q_ref[:, sl]
        kc = kc_ref[:, sl].astype(BF16)
        vc = vc_ref[:, sl].astype(BF16)
        kn = kn_ref[:, sl]
        vn = vn_ref[:, sl]
        qs = (jnp.where(lo_q, qp, zq), jnp.where(lo_q, zq, qp))
        pcs, pns, ls = [], [], []
        for hh in range(2):
            s_c = jnp.where(val_c, _dot_nt(qs[hh], kc), NEG)
            s_n = jnp.where(val_n, _dot_nt(qs[hh], kn), NEG)
            m = jnp.maximum(jnp.max(s_c, axis=-1, keepdims=True), jnp.max(s_n, axis=-1, keepdims=True))
            p_c = jnp.exp(s_c - m) * cnt_c
            p_n = jnp.exp(s_n - m) * cnt_n
            ls.append(jnp.sum(p_c, axis=-1, keepdims=True) + jnp.sum(p_n, axis=-1, keepdims=True))
            pcs.append(p_c.astype(BF16))
            pns.append(p_n.astype(BF16))
        vbd_c = jnp.concatenate([jnp.where(lo_c, vc, zc), jnp.where(lo_c, zc, vc)], axis=0)
        vbd_n = jnp.concatenate([jnp.where(lo_q, vn, zq), jnp.where(lo_q, zq, vn)], axis=0)
        pv = _dot(jnp.concatenate(pcs, axis=1), vbd_c) + _dot(jnp.concatenate(pns, axis=1), vbd_n)
        parts.append(pv / jnp.where(lo_q, ls[0], ls[1]))
    o = jnp.concatenate(parts, axis=1)
    o_ref[...] = _rms(o, g_ref[...]).astype(BF16)


def _attn_sample(q, k_cache, v_cache, k_new, v_new, g):
    bsz, t, _ = q.shape
    past = k_cache.shape[1]
    new = pl.BlockSpec((None, t, WIDTH), lambda b: (b, 0, 0))
    old = pl.BlockSpec((None, past, WIDTH), lambda b: (b, 0, 0))
    kern = functools.partial(_attn_sample_kernel, t=t, past=past)
    return pl.pallas_call(
        kern,
        grid=(bsz,),
        in_specs=[new, old, old, new, new, _const_spec((1, WIDTH))],
        out_specs=new,
        out_shape=jax.ShapeDtypeStruct((bsz, t, WIDTH), BF16),
        compiler_params=_params(("parallel",)),
        name="attn_sample",
    )(q, k_cache, v_cache, k_new, v_new, g.reshape(1, WIDTH))


def _outproj_kernel(yl_ref, oa_ref, x_ref, wt_ref, wb_ref, g_ref, wq_ref, h_ref, xn_ref, qp_ref):
    h = x_ref[...] + (_dot(yl_ref[...], wt_ref[...]) + _dot(oa_ref[...], wb_ref[...]))
    h_ref[...] = h
    xn = _rms(h, g_ref[...])
    xn_ref[...] = xn
    qp_ref[...] = _dot(xn.astype(BF16), wq_ref[...])


def _outproj(yl, oa, x, w_out_bf, g_ffn, wq_bf):
    n = x.shape[0]
    tm = min(TOKEN_TILE, n)
    nq = wq_bf.shape[1]
    row = lambda d: pl.BlockSpec((tm, d), lambda i: (i, 0))
    return pl.pallas_call(
        _outproj_kernel,
        grid=(n // tm,),
        in_specs=[row(WIDTH), row(WIDTH), row(D_MODEL),
                  _const_spec((WIDTH, D_MODEL)), _const_spec((WIDTH, D_MODEL)),
                  _const_spec((1, D_MODEL)), _const_spec(wq_bf.shape)],
        out_specs=[row(D_MODEL), row(D_MODEL), row(nq)],
        out_shape=[jax.ShapeDtypeStruct((n, D_MODEL), F32),
                   jax.ShapeDtypeStruct((n, D_MODEL), F32),
                   jax.ShapeDtypeStruct((n, nq), F32)],
        compiler_params=_params(("parallel",)),
        name="outproj",
    )(yl, oa, x, w_out_bf[:WIDTH], w_out_bf[WIDTH:], g_ffn.reshape(1, D_MODEL), wq_bf)


def _first_max(vals, tags, extra=None):
    pay = [tags] + ([extra] if extra is not None else [])
    while len(vals) > 1:
        nv, npay = [], [[] for _ in pay]
        for a in range(0, len(vals) - 1, 2):
            take_b = vals[a + 1] > vals[a]
            nv.append(jnp.maximum(vals[a], vals[a + 1]))
            for k, p in enumerate(pay):
                npay[k].append(jnp.where(take_b, p[a + 1], p[a]))
        if len(vals) % 2:
            nv.append(vals[-1])
            for k, p in enumerate(pay):
                npay[k].append(p[-1])
        vals, pay = nv, npay
    v = vals[0]
    pay = [p[0] for p in pay]
    for d in (4, 2, 1):
        vr = pltpu.roll(v, d, axis=0)
        pr = [pltpu.roll(p, d, axis=0) for p in pay]
        better = (vr > v) | ((vr == v) & (pr[0] < pay[0]))
        v = jnp.where(better, vr, v)
        pay = [jnp.where(better, a, b) for a, b in zip(pr, pay)]
    return (v, *pay)


def _topk_kernel(qp_ref, kh_ref, kl_ref, idx_ref, g_ref, top_s, top_i, best_s, out_g, out_i, *, tn):
    sub = lax.broadcasted_iota(I32, (SUBLANES, tn), 0)
    subf = sub.astype(F32)
    nslab = PEER_KEYS // SUBLANES
    key_tags = [subf + float(SUBLANES * k) for k in range(nslab)]
    ninf = jnp.full((SUBLANES, tn), -jnp.inf, F32)
    for h in range(PEER_HEADS):
        for c in range(2):
            hc = 2 * h + c
            q_hi, q_lo = _split(qp_ref[:, hc * PEER_HALF:(hc + 1) * PEER_HALF])
            kh = kh_ref[hc]
            s = _dot_nt(kh, q_hi) + _dot_nt(kh, q_lo) + _dot_nt(kl_ref[hc], q_hi)
            slabs = [s[SUBLANES * k:SUBLANES * (k + 1), :] for k in range(nslab)]
            for r in range(PEER_TOPK):
                m, am = _first_max(slabs, key_tags)
                slabs = [jnp.where(t == am, ninf, x) for x, t in zip(slabs, key_tags)]
                top_s[c, r:r + 1, :] = m[0:1, :]
                top_i[c, r:r + 1, :] = am[0:1, :]
        s0_lo, s0_hi = top_s[0, 0:SUBLANES, :], top_s[0, SUBLANES:2 * SUBLANES, :]
        i0_lo, i0_hi = top_i[0, 0:SUBLANES, :], top_i[0, SUBLANES:2 * SUBLANES, :]
        s1_lo, s1_hi = top_s[1, 0:SUBLANES, :], top_s[1, SUBLANES:2 * SUBLANES, :]
        i1_lo, i1_hi = top_i[1, 0:SUBLANES, :], top_i[1, SUBLANES:2 * SUBLANES, :]
        bcast = lambda x, a: jnp.broadcast_to(x[a:a + 1, :], (SUBLANES, tn))
        cs = [bcast(s0_lo, 0) + s1_lo, bcast(s0_lo, 0) + s1_hi]
        ci = [bcast(i0_lo, 0) * PEER_KEYS + i1_lo, bcast(i0_lo, 0) * PEER_KEYS + i1_hi]
        pos = [subf, subf + float(SUBLANES)]
        for a in range(1, SUBLANES):
            nb = PEER_TOPK // (a + 1)
            cs.append(jnp.where(sub < nb, bcast(s0_lo, a) + s1_lo, ninf))
            ci.append(bcast(i0_lo, a) * PEER_KEYS + i1_lo)
            pos.append(subf + float(a * PEER_TOPK))
        cs.append(s0_hi + bcast(s1_lo, 0))
        ci.append(i0_hi * PEER_KEYS + bcast(i1_lo, 0))
        pos.append((subf + float(SUBLANES)) * PEER_TOPK)
        for r in range(PEER_TOPK):
            m, p, e = _first_max(cs, pos, ci)
            cs = [jnp.where(t == p, ninf, x) for x, t in zip(cs, pos)]
            row = h * PEER_TOPK + r
            out_i[row:row + 1, :] = e[0:1, :]
            best_s[r:r + 1, :] = m[0:1, :]
        bs = best_s[...]
        ex = jnp.exp(bs - bs[0:1, :])
        out_g[h * PEER_TOPK:(h + 1) * PEER_TOPK, :] = ex / jnp.sum(ex, axis=0, keepdims=True)
    g_ref[...] = out_g[...].T
    idx_ref[...] = out_i[...].T.astype(I32)


def _topk(qp, sub_keys):
    n = qp.shape[0]
    tn = min(TOPK_TILE, n)
    kh, kl = _split(sub_keys.reshape(PEER_HEADS * 2, PEER_KEYS, PEER_HALF))
    kern = functools.partial(_topk_kernel, tn=tn)
    return pl.pallas_call(
        kern,
        grid=(n // tn,),
        in_specs=[pl.BlockSpec((tn, qp.shape[1]), lambda i: (i, 0)),
                  _const_spec(kh.shape), _const_spec(kl.shape)],
        out_specs=[pl.BlockSpec((tn, PEER_SEL), lambda i: (i, 0))] * 2,
        out_shape=[jax.ShapeDtypeStruct((n, PEER_SEL), I32),
                   jax.ShapeDtypeStruct((n, PEER_SEL), F32)],
        scratch_shapes=[pltpu.VMEM((2, PEER_TOPK, tn), F32),
                        pltpu.VMEM((2, PEER_TOPK, tn), F32),
                        pltpu.VMEM((PEER_TOPK, tn), F32),
                        pltpu.VMEM((PEER_SEL, tn), F32),
                        pltpu.VMEM((PEER_SEL, tn), F32)],
        compiler_params=_params(("parallel",)),
        name="peer_topk",
    )(qp, kh, kl)


def _peer_kernel(idx_ref, nxt_ref, xn_ref, g_ref, tab_ref, f_ref, buf, sem, a_s, c_s, *, tb):
    s = pl.program_id(0)
    n = pl.num_programs(0)
    slot = lax.rem(s, 2)
    other = 1 - slot
    ncol = PEER_SEL * SUBLANES

    def row_copy(e, slot_, t, r):
        return pltpu.make_async_copy(tab_ref.at[e], buf.at[slot_, t, r], sem.at[slot_, t])

    def issue(ids, slot_, t, r0, r1):
        for r in range(r0, r1):
            row_copy(ids[t, r], slot_, t, r).start(priority=r % 2)

    def drain(slot_, t):
        for r in range(PEER_SEL):
            row_copy(0, slot_, t, r).wait()

    @pl.when(s == 0)
    def _():
        def tok(t, carry):
            issue(idx_ref, 0, t, 0, PEER_SEL)
            return carry
        lax.fori_loop(0, tb, tok, 0)

    diag = (lax.broadcasted_iota(I32, (SUBLANES, ncol), 1) & (SUBLANES - 1)) == \
        lax.broadcasted_iota(I32, (SUBLANES, ncol), 0)

    for t in range(tb):
        drain(slot, t)
        x8 = xn_ref[t].astype(BF16)
        u2 = buf[slot, t, :, 0:SUBLANES, :].reshape(ncol, LANES).astype(BF16)
        m = _dot_nt(x8, u2)
        a_s[t:t + 1, :] = jnp.sum(jnp.where(diag, m, 0.0), axis=0, keepdims=True)
        issue(nxt_ref, other, t, 0, PEER_SEL // 2)

    lane = lax.broadcasted_iota(I32, (tb, LANES), 1)
    cols = []
    for c in range(ncol // LANES):
        x = a_s[:, c * LANES:(c + 1) * LANES]
        for k in (1, 2, 4):
            x = x + jnp.where((lane & k) != 0, pltpu.roll(x, k, axis=1),
                              pltpu.roll(x, LANES - k, axis=1))
        cols.append(x)
    act = jnp.concatenate(cols, axis=1)
    expand = (lax.broadcasted_iota(I32, (PEER_SEL, ncol), 1) >> 3) == \
        lax.broadcasted_iota(I32, (PEER_SEL, ncol), 0)
    expand = jnp.where(expand, 1.0, 0.0).astype(BF16)
    g_hi, g_lo = _split(g_ref[...])
    g_rep = _dot(g_hi, expand) + _dot(g_lo, expand)
    c_s[...] = g_rep * jax.nn.gelu(act)

    for t in range(tb):
        c8 = jnp.where(diag, jnp.broadcast_to(c_s[t:t + 1, :], (SUBLANES, ncol)), 0.0).astype(BF16)
        v2 = buf[slot, t, :, SUBLANES:2 * SUBLANES, :].reshape(ncol, LANES).astype(BF16)
        f_ref[t] = _dot(c8, v2)
        issue(nxt_ref, other, t, PEER_SEL // 2, PEER_SEL)

    @pl.when(s == n - 1)
    def _():
        def tok(t, carry):
            drain(other, t)
            return carry
        lax.fori_loop(0, tb, tok, 0)


def _pack_kernel(u_ref, v_ref, o_ref):
    for s in range(SUBLANES):
        o_ref[:, s, :] = u_ref[:, s * LANES:(s + 1) * LANES]
        o_ref[:, SUBLANES + s, :] = v_ref[:, s * LANES:(s + 1) * LANES]


def _pack_table(u, v):
    e = u.shape[0]
    te = min(PACK_TILE, e)
    row = pl.BlockSpec((te, D_MODEL), lambda i: (i, 0))
    return pl.pallas_call(
        _pack_kernel,
        grid=(e // te,),
        in_specs=[row, row],
        out_specs=pl.BlockSpec((te, 2 * SUBLANES, LANES), lambda i: (i, 0, 0)),
        out_shape=jax.ShapeDtypeStruct((e, 2 * SUBLANES, LANES), F32),
        compiler_params=_params(("parallel",)),
        name="peer_pack",
    )(u, v)


def _peer(idx, g, xn, table):
    n = xn.shape[0]
    tb = PEER_TILE
    steps = n // tb
    kern = functools.partial(_peer_kernel, tb=tb)
    smem = lambda fn: pl.BlockSpec((tb, PEER_SEL), fn, memory_space=pltpu.SMEM)
    slab = pl.BlockSpec((tb, SUBLANES, LANES), lambda s: (s, 0, 0))
    out = pl.pallas_call(
        kern,
        grid=(steps,),
        in_specs=[smem(lambda s: (s, 0)),
                  smem(lambda s: (jnp.minimum(s + 1, steps - 1), 0)),
                  slab,
                  pl.BlockSpec((tb, PEER_SEL), lambda s: (s, 0)),
                  pl.BlockSpec(memory_space=pl.ANY)],
        out_specs=slab,
        out_shape=jax.ShapeDtypeStruct((n, SUBLANES, LANES), F32),
        scratch_shapes=[pltpu.VMEM((2, tb, PEER_SEL, 2 * SUBLANES, LANES), F32),
                        pltpu.SemaphoreType.DMA((2, tb)),
                        pltpu.VMEM((tb, PEER_SEL * SUBLANES), F32),
                        pltpu.VMEM((tb, PEER_SEL * SUBLANES), F32)],
        compiler_params=_params(("arbitrary",)),
        name="peer_gather",
    )(idx, idx, xn.reshape(n, SUBLANES, LANES), g, table)
    return out.reshape(n, D_MODEL)


def _final_kernel(h_ref, f_ref, p_ref, gp_ref, wg_ref, wp_ref, gf_ref, y_ref):
    h = h_ref[...] + f_ref[...]
    gate = jax.nn.sigmoid(_dot(_rms(h, gp_ref[...]).astype(BF16), wg_ref[...]))
    h = h + gate * _dot(p_ref[...].astype(BF16), wp_ref[...])
    y_ref[...] = _rms(h, gf_ref[...])


def _final(h, f, p, g_ple, wg_bf, wp_bf, g_final):
    n = h.shape[0]
    tm = min(TOKEN_TILE, n)
    row = lambda d: pl.BlockSpec((tm, d), lambda i: (i, 0))
    return pl.pallas_call(
        _final_kernel,
        grid=(n // tm,),
        in_specs=[row(D_MODEL), row(D_MODEL), row(p.shape[1]), _const_spec((1, D_MODEL)),
                  _const_spec(wg_bf.shape), _const_spec(wp_bf.shape), _const_spec((1, D_MODEL))],
        out_specs=row(D_MODEL),
        out_shape=jax.ShapeDtypeStruct((n, D_MODEL), F32),
        compiler_params=_params(("parallel",)),
        name="ple_final",
    )(h, f, p, g_ple.reshape(1, D_MODEL), wg_bf, wp_bf, g_final.reshape(1, D_MODEL))


def _trunk(x, p, conv_state, lru_state, win_k, win_v, w):
    bsz, t, _ = x.shape
    n = bsz * t
    xb, gate, q, k, v, kb, vb = _inproj(x.reshape(n, D_MODEL), w["g_mix"], w["w_in"])
    seq = lambda a: a.reshape(bsz, t, WIDTH)
    xb3 = seq(xb)
    if conv_state is None:
        conv_state = jnp.zeros((bsz, 3, WIDTH), F32)
        lru_state = jnp.zeros((bsz, WIDTH), F32)
    yl, h_last = _lru(xb3, seq(gate), conv_state, lru_state, w["conv_w"], w["conv_b"],
                      w["w_ra"], w["b_ra"], w["w_ri"], w["b_ri"], w["lam"], w["g_out_lru"])
    if win_k is None:
        oa = _attn_prompt(seq(q), seq(kb), seq(vb), w["g_out_attn"])
        keep = min(MAX_WINDOW, t)
        new_k, new_v = seq(k)[:, t - keep:], seq(v)[:, t - keep:]
    else:
        past = win_k.shape[1]
        oa = _attn_sample(seq(q), win_k.reshape(bsz, past, WIDTH), win_v.reshape(bsz, past, WIDTH),
                          seq(kb), seq(vb), w["g_out_attn"])
        new_k, new_v = seq(k), seq(v)
    h1, xn2, qp = _outproj(yl.reshape(n, WIDTH), oa.reshape(n, WIDTH), x.reshape(n, D_MODEL),
                           w["w_out"], w["g_ffn"], w["w_peer_q"])
    idx, g = _topk(qp, w["sub_keys"])
    f = _peer(idx, g, xn2, w["table"])
    y = _final(h1, f, p.reshape(n, p.shape[-1]), w["g_ple"], w["w_ple_gate"], w["w_ple_proj"],
               w["g_final"])
    heads = lambda a: a.reshape(1, bsz, a.shape[1], ATTN_HEADS, HEAD_DIM)
    return (y.reshape(bsz, t, D_MODEL), xb3[:, t - 3:][None], h_last.reshape(1, bsz, WIDTH),
            heads(new_k), heads(new_v))


def kernel(x_prompt, x_sample, state_conv, state_lru, cache_win_k, cache_win_v, p_prompt, p_sample,
           g_mix, w_in, conv_w, conv_b, w_ra, b_ra, w_ri, b_ri, lru_lambda, g_out_lru, g_out_attn,
           w_out, g_ffn, w_peer_q, peer_sub_keys, peer_u, peer_v, g_ple, w_ple_gate, w_ple_proj,
           g_final):
    w = {
        "g_mix": g_mix[0].reshape(1, D_MODEL), "w_in": w_in[0].astype(BF16),
        "conv_w": conv_w[0], "conv_b": conv_b[0], "w_ra": w_ra[0], "b_ra": b_ra[0],
        "w_ri": w_ri[0], "b_ri": b_ri[0], "lam": lru_lambda[0],
        "g_out_lru": g_out_lru[0], "g_out_attn": g_out_attn[0],
        "w_out": w_out[0].astype(BF16), "g_ffn": g_ffn[0], "w_peer_q": w_peer_q[0].astype(BF16),
        "sub_keys": peer_sub_keys[0],
        "table": _pack_table(peer_u[0], peer_v[0]),
        "g_ple": g_ple[0], "w_ple_gate": w_ple_gate[0].astype(BF16),
        "w_ple_proj": w_ple_proj[0].astype(BF16), "g_final": g_final,
    }
    yp, pc, plru, pk, pv = _trunk(x_prompt, p_prompt[0], None, None, None, None, w)
    ys, sc, slru, sk, sv = _trunk(x_sample, p_sample[0], state_conv[0], state_lru[0],
                                  cache_win_k[0], cache_win_v[0], w)
    return (yp, ys, pc, plru, pk, pv, sc, slru, sk, sv)
```

```python
import functools

import jax
import jax.numpy as jnp
from jax import lax
from jax.experimental import pallas as pl
from jax.experimental.pallas import tpu as pltpu

F32 = jnp.float32
BF16 = jnp.bfloat16
I32 = jnp.int32

EPS = 1e-6
D_MODEL = 1024
HEAD_DIM = 64
ATTN_HEADS = 8
WIDTH = 512
LRU_C = 8.0
DILATED_GROUPS = ((128, 1), (512, 4), (2048, 16))
MAX_WINDOW = 2048
PEER_HEADS = 8
PEER_KEYS = 128
PEER_HALF = 128
PEER_TOPK = 16
PEER_SEL = PEER_HEADS * PEER_TOPK
NEG = -1e30

LANES = 128
SUBLANES = 8
VMEM_LIMIT = 48 * 1024 * 1024

TOKEN_TILE = 512
LRU_TILE = 256
ATTN_TILE = 256
TOPK_TILE = 128
PEER_TILE = 16
PACK_TILE = 256


def _params(sem):
    return pltpu.CompilerParams(dimension_semantics=sem, vmem_limit_bytes=VMEM_LIMIT)


def _rms(x, g):
    return x * lax.rsqrt(jnp.mean(x * x, axis=-1, keepdims=True) + EPS) * g


def _dot(a, b):
    return jnp.dot(a, b, preferred_element_type=F32)


def _dot_nt(a, b):
    return lax.dot_general(a, b, (((1,), (1,)), ((), ())), preferred_element_type=F32)


def _split(a):
    hi = a.astype(BF16)
    lo = (a - hi.astype(F32)).astype(BF16)
    return hi, lo


def _const_spec(shape):
    nd = len(shape)
    return pl.BlockSpec(shape, lambda *_: (0,) * nd)


def _inproj_kernel(x_ref, g_ref, w_ref, xb_ref, gate_ref, q_ref, k_ref, v_ref, kb_ref, vb_ref):
    xn = _rms(x_ref[...], g_ref[...]).astype(BF16)

    def proj(c):
        return _dot(xn, w_ref[:, c * WIDTH:(c + 1) * WIDTH])

    xb_ref[...] = proj(0)
    gate_ref[...] = proj(1)
    q_ref[...] = (proj(2) * (HEAD_DIM ** -0.5)).astype(BF16)
    k = proj(3)
    k_ref[...] = k
    kb_ref[...] = k.astype(BF16)
    v = proj(4)
    v_ref[...] = v
    vb_ref[...] = v.astype(BF16)


def _inproj(x, g, w_bf):
    n = x.shape[0]
    tm = min(TOKEN_TILE, n)
    row = lambda d: pl.BlockSpec((tm, d), lambda i: (i, 0))
    f = jax.ShapeDtypeStruct((n, WIDTH), F32)
    b = jax.ShapeDtypeStruct((n, WIDTH), BF16)
    return pl.pallas_call(
        _inproj_kernel,
        grid=(n // tm,),
        in_specs=[row(D_MODEL), _const_spec((1, D_MODEL)), _const_spec(w_bf.shape)],
        out_specs=[row(WIDTH)] * 7,
        out_shape=[f, f, b, f, f, b, b],
        compiler_params=_params(("parallel",)),
        name="inproj",
    )(x, g, w_bf)


def _lru_kernel(xb_ref, gate_ref, cs_ref, h0_ref, cw_ref, cb_ref, wah_ref, wal_ref, ba_ref,
                wih_ref, wil_ref, bi_ref, lam_ref, g_ref, yl_ref, hl_ref,
                xpad, hc, a_s, b_s, h_s, *, tb):
    j = pl.program_id(1)

    @pl.when(j == 0)
    def _():
        xpad[0:SUBLANES, :] = jnp.zeros((SUBLANES, WIDTH), F32)
        xpad[SUBLANES - 3:SUBLANES, :] = cs_ref[...]
        hc[...] = jnp.broadcast_to(h0_ref[...], (SUBLANES, WIDTH))

    x = xb_ref[...]
    xpad[SUBLANES:SUBLANES + tb, :] = x
    w = cw_ref[...]
    xc = cb_ref[...] + xpad[SUBLANES - 3:SUBLANES - 3 + tb, :] * w[0:1]
    xc = xc + xpad[SUBLANES - 2:SUBLANES - 2 + tb, :] * w[1:2]
    xc = xc + xpad[SUBLANES - 1:SUBLANES - 1 + tb, :] * w[2:3]
    xc = xc + x * w[3:4]
    xpad[0:SUBLANES, :] = xpad[tb:tb + SUBLANES, :]

    x_hi, x_lo = _split(xc)

    def gate_dot(wh_ref, wl_ref):
        wh = wh_ref[...]
        return _dot(x_hi, wh) + _dot(x_lo, wh) + _dot(x_hi, wl_ref[...])

    r = jax.nn.sigmoid(gate_dot(wah_ref, wal_ref) + ba_ref[...])
    i = jax.nn.sigmoid(gate_dot(wih_ref, wil_ref) + bi_ref[...])
    nl = -lam_ref[...]
    softplus = jnp.maximum(nl, 0.0) + jnp.log1p(jnp.exp(-jnp.abs(nl)))
    log_a = -LRU_C * r * softplus
    a = jnp.exp(log_a)
    a_s[...] = a
    b_s[...] = jnp.sqrt(jnp.tanh(-log_a) * (a * a + 1.0)) * (i * xc)

    rows = lax.broadcasted_iota(I32, (SUBLANES, WIDTH), 0)

    def group(gi, hb):
        off = pl.multiple_of(gi * SUBLANES, SUBLANES)
        av = a_s[pl.ds(off, SUBLANES), :]
        bv = b_s[pl.ds(off, SUBLANES), :]
        for d in (1, 2, 4):
            a_sh = pltpu.roll(av, d, axis=0)
            b_sh = pltpu.roll(bv, d, axis=0)
            m = rows >= d
            bv = jnp.where(m, av * b_sh + bv, bv)
            av = jnp.where(m, av * a_sh, av)
        h = av * hb + bv
        h_s[pl.ds(off, SUBLANES), :] = h
        return jnp.broadcast_to(h[SUBLANES - 1:SUBLANES, :], (SUBLANES, WIDTH))

    hb = lax.fori_loop(0, tb // SUBLANES, group, hc[...])
    hc[...] = hb

    y = h_s[...] * jax.nn.gelu(gate_ref[...])
    yl_ref[...] = _rms(y, g_ref[...]).astype(BF16)

    @pl.when(j == pl.num_programs(1) - 1)
    def _():
        hl_ref[...] = hb[0:1, :]


def _lru(xb, gate, conv_state, h0, conv_w, conv_b, w_ra, b_ra, w_ri, b_ri, lam, g_out):
    bsz, t, _ = xb.shape
    tb = min(LRU_TILE, t)
    wah, wal = _split(jax.scipy.linalg.block_diag(*w_ra))
    wih, wil = _split(jax.scipy.linalg.block_diag(*w_ri))
    seq = pl.BlockSpec((None, tb, WIDTH), lambda b, j: (b, j, 0))
    vec = _const_spec((1, WIDTH))
    sq = _const_spec((WIDTH, WIDTH))
    kern = functools.partial(_lru_kernel, tb=tb)
    return pl.pallas_call(
        kern,
        grid=(bsz, t // tb),
        in_specs=[seq, seq,
                  pl.BlockSpec((None, 3, WIDTH), lambda b, j: (b, 0, 0)),
                  pl.BlockSpec((None, 1, WIDTH), lambda b, j: (b, 0, 0)),
                  _const_spec((4, WIDTH)), vec, sq, sq, vec, sq, sq, vec, vec, vec],
        out_specs=[seq, pl.BlockSpec((None, 1, WIDTH), lambda b, j: (b, 0, 0))],
        out_shape=[jax.ShapeDtypeStruct((bsz, t, WIDTH), BF16),
                   jax.ShapeDtypeStruct((bsz, 1, WIDTH), F32)],
        scratch_shapes=[pltpu.VMEM((tb + SUBLANES, WIDTH), F32),
                        pltpu.VMEM((SUBLANES, WIDTH), F32),
                        pltpu.VMEM((tb, WIDTH), F32),
                        pltpu.VMEM((tb, WIDTH), F32),
                        pltpu.VMEM((tb, WIDTH), F32)],
        compiler_params=_params(("parallel", "arbitrary")),
        name="rg_lru",
    )(xb, gate, conv_state, h0.reshape(bsz, 1, WIDTH), conv_w, conv_b.reshape(1, WIDTH),
      wah, wal, b_ra.reshape(1, WIDTH), wih, wil, b_ri.reshape(1, WIDTH),
      lam.reshape(1, WIDTH), g_out.reshape(1, WIDTH))


LOCAL_WINDOW = 512
FAR_WINDOW, FAR_DIL = DILATED_GROUPS[-1]
assert all(w <= LOCAL_WINDOW for w, _ in DILATED_GROUPS[:-1]) and LOCAL_WINDOW % FAR_DIL == 0


def _tap_count(d, limit=MAX_WINDOW):
    one = jnp.ones(d.shape, F32)
    zero = jnp.zeros(d.shape, F32)
    cnt = zero
    for window, dil in DILATED_GROUPS:
        hit = jnp.where(d <= min(window, limit), one, zero)
        if dil > 1:
            hit = jnp.where((d & (dil - 1)) == 0, hit, zero)
        cnt = cnt + hit
    return jnp.where(d >= 0, cnt, zero)


def _pair_masks(rows):
    lane = lax.broadcasted_iota(I32, (rows, LANES), 1)
    return lane < HEAD_DIM


def _attn_block(q_ref, k_ref, v_ref, cnt, m_s, l_s, acc_s, rows):
    lo = _pair_masks(rows)
    valid = cnt > 0.0
    zb = jnp.zeros((rows, LANES), BF16)
    for hp in range(ATTN_HEADS // 2):
        sl = slice(hp * LANES, (hp + 1) * LANES)
        qp = q_ref[:, sl]
        kp = k_ref[:, sl]
        vp = v_ref[:, sl]
        qs = (jnp.where(lo, qp, zb), jnp.where(lo, zb, qp))
        ps, alphas = [], []
        for hh in range(2):
            h = 2 * hp + hh
            s = jnp.where(valid, _dot_nt(qs[hh], kp), NEG)
            m_old = m_s[h][:, 0:1]
            m_new = jnp.maximum(m_old, jnp.max(s, axis=-1, keepdims=True))
            p = jnp.exp(s - m_new) * cnt
            alpha = jnp.exp(m_old - m_new)
            l_new = alpha * l_s[h][:, 0:1] + jnp.sum(p, axis=-1, keepdims=True)
            m_s[h] = jnp.broadcast_to(m_new, (rows, LANES))
            l_s[h] = jnp.broadcast_to(l_new, (rows, LANES))
            ps.append(p.astype(BF16))
            alphas.append(alpha)
        pcat = jnp.concatenate(ps, axis=1)
        vbd = jnp.concatenate([jnp.where(lo, vp, zb), jnp.where(lo, zb, vp)], axis=0)
        alpha_pair = jnp.where(lo, alphas[0], alphas[1])
        acc_s[:, sl] = alpha_pair * acc_s[:, sl] + _dot(pcat, vbd)


def _attn_init(m_s, l_s, acc_s):
    m_s[...] = jnp.full(m_s.shape, NEG, F32)
    l_s[...] = jnp.zeros(l_s.shape, F32)
    acc_s[...] = jnp.zeros(acc_s.shape, F32)


def _per_head(ref, lo):
    return jnp.concatenate([jnp.where(lo, ref[2 * hp][:, 0:1], ref[2 * hp + 1][:, 0:1])
                            for hp in range(ATTN_HEADS // 2)], axis=1)


def _attn_local_kernel(q_ref, k_ref, v_ref, acc_ref, m_ref, l_ref, m_s, l_s, acc_s, *, tq, nkb):
    i = pl.program_id(1)
    j = pl.program_id(2)

    @pl.when(j == 0)
    def _():
        _attn_init(m_s, l_s, acc_s)

    kb = i - (nkb - 1) + j

    @pl.when(kb >= 0)
    def _():
        r = lax.broadcasted_iota(I32, (tq, tq), 0)
        c = lax.broadcasted_iota(I32, (tq, tq), 1)
        cnt = _tap_count((i - kb) * tq + r - c, LOCAL_WINDOW)
        _attn_block(q_ref, k_ref, v_ref, cnt, m_s, l_s, acc_s, tq)

    @pl.when(j == nkb - 1)
    def _():
        lo = _pair_masks(tq)
        acc_ref[...] = acc_s[...]
        m_ref[...] = _per_head(m_s, lo)
        l_ref[...] = _per_head(l_s, lo)


def _attn_far_kernel(q_ref, k_ref, v_ref, acc_ref, m_ref, l_ref, g_ref, o_ref, m_s, l_s, acc_s,
                     *, td):
    i = pl.program_id(2)
    j = pl.program_id(3)

    @pl.when(j == 0)
    def _():
        _attn_init(m_s, l_s, acc_s)

    kb = i - 1 + j

    @pl.when(kb >= 0)
    def _():
        r = lax.broadcasted_iota(I32, (td, td), 0)
        c = lax.broadcasted_iota(I32, (td, td), 1)
        d = (i - kb) * td + r - c
        far = jnp.where(d > LOCAL_WINDOW // FAR_DIL, 1.0, 0.0)
        cnt = jnp.where(d <= FAR_WINDOW // FAR_DIL, far, 0.0)
        _attn_block(q_ref, k_ref, v_ref, cnt, m_s, l_s, acc_s, td)

    @pl.when(j == 1)
    def _():
        lo = _pair_masks(td)
        m_far = _per_head(m_s, lo)
        m_loc = m_ref[...]
        m_tot = jnp.maximum(m_loc, m_far)
        w_loc = jnp.exp(m_loc - m_tot)
        w_far = jnp.exp(m_far - m_tot)
        num = acc_ref[...] * w_loc + acc_s[...] * w_far
        den = l_ref[...] * w_loc + _per_head(l_s, lo) * w_far
        o_ref[...] = _rms(num / den, g_ref[...]).astype(BF16)


def _attn_prompt(q, k, v, g):
    bsz, t, _ = q.shape
    tq = min(ATTN_TILE, t)
    nkb = LOCAL_WINDOW // tq + 1
    qspec = pl.BlockSpec((None, tq, WIDTH), lambda b, i, j: (b, i, 0))
    kspec = pl.BlockSpec((None, tq, WIDTH), lambda b, i, j: (b, jnp.maximum(i - (nkb - 1) + j, 0), 0))
    stat = jax.ShapeDtypeStruct((bsz, t, WIDTH), F32)
    acc, m, l = pl.pallas_call(
        functools.partial(_attn_local_kernel, tq=tq, nkb=nkb),
        grid=(bsz, t // tq, nkb),
        in_specs=[qspec, kspec, kspec],
        out_specs=[qspec] * 3,
        out_shape=[stat] * 3,
        scratch_shapes=[pltpu.VMEM((ATTN_HEADS, tq, LANES), F32),
                        pltpu.VMEM((ATTN_HEADS, tq, LANES), F32),
                        pltpu.VMEM((tq, WIDTH), F32)],
        compiler_params=_params(("parallel", "parallel", "arbitrary")),
        name="attn_local",
    )(q, k, v)

    tr = t // FAR_DIL
    td = min(ATTN_TILE, tr)
    assert td >= min(FAR_WINDOW // FAR_DIL, tr)
    view = lambda a: a.reshape(bsz, tr, FAR_DIL * WIDTH)
    dq = pl.BlockSpec((None, td, WIDTH), lambda b, r, i, j: (b, i, r))
    dk = pl.BlockSpec((None, td, WIDTH), lambda b, r, i, j: (b, jnp.maximum(i - 1 + j, 0), r))
    out = pl.pallas_call(
        functools.partial(_attn_far_kernel, td=td),
        grid=(bsz, FAR_DIL, tr // td, 2),
        in_specs=[dq, dk, dk, dq, dq, dq, _const_spec((1, WIDTH))],
        out_specs=dq,
        out_shape=jax.ShapeDtypeStruct((bsz, tr, FAR_DIL * WIDTH), BF16),
        scratch_shapes=[pltpu.VMEM((ATTN_HEADS, td, LANES), F32),
                        pltpu.VMEM((ATTN_HEADS, td, LANES), F32),
                        pltpu.VMEM((td, WIDTH), F32)],
        compiler_params=_params(("parallel", "parallel", "parallel", "arbitrary")),
        name="attn_far",
    )(view(q), view(k), view(v), view(acc), view(m), view(l), g.reshape(1, WIDTH))
    return out.reshape(bsz, t, WIDTH)


def _attn_sample_kernel(q_ref, kc_ref, vc_ref, kn_ref, vn_ref, g_ref, o_ref, *, t, past):
    lo_q = _pair_masks(t)
    lo_c = _pair_masks(past)
    rq = lax.broadcasted_iota(I32, (t, past), 0)
    cp = lax.broadcasted_iota(I32, (t, past), 1)
    cnt_c = _tap_count(past + rq - cp)
    r2 = lax.broadcasted_iota(I32, (t, t), 0)
    c2 = lax.broadcasted_iota(I32, (t, t), 1)
    cnt_n = _tap_count(r2 - c2)
    val_c = cnt_c > 0.0
    val_n = cnt_n > 0.0
    zq = jnp.zeros((t, LANES), BF16)
    zc = jnp.zeros((past, LANES), BF16)
    parts = []
    for hp in range(ATTN_HEADS // 2):
        sl = slice(hp * LANES, (hp + 1) * LANES)
        qp = q_ref[:, sl]
        kc = kc_ref[:, sl].astype(BF16)
        vc = vc_ref[:, sl].astype(BF16)
        kn = kn_ref[:, sl]
        vn = vn_ref[:, sl]
        qs = (jnp.where(lo_q, qp, zq), jnp.where(lo_q, zq, qp))
        pcs, pns, ls = [], [], []
        for hh in range(2):
            s_c = jnp.where(val_c, _dot_nt(qs[hh], kc), NEG)
            s_n = jnp.where(val_n, _dot_nt(qs[hh], kn), NEG)
            m = jnp.maximum(jnp.max(s_c, axis=-1, keepdims=True), jnp.max(s_n, axis=-1, keepdims=True))
            p_c = jnp.exp(s_c - m) * cnt_c
            p_n = jnp.exp(s_n - m) * cnt_n
            ls.append(jnp.sum(p_c, axis=-1, keepdims=True) + jnp.sum(p_n, axis=-1, keepdims=True))
            pcs.append(p_c.astype(BF16))
            pns.append(p_n.astype(BF16))
        vbd_c = jnp.concatenate([jnp.where(lo_c, vc, zc), jnp.where(lo_c, zc, vc)], axis=0)
        vbd_n = jnp.concatenate([jnp.where(lo_q, vn, zq), jnp.where(lo_q, zq, vn)], axis=0)
        pv = _dot(jnp.concatenate(pcs, axis=1), vbd_c) + _dot(jnp.concatenate(pns, axis=1), vbd_n)
        parts.append(pv / jnp.where(lo_q, ls[0], ls[1]))
    o = jnp.concatenate(parts, axis=1)
    o_ref[...] = _rms(o, g_ref[...]).astype(BF16)


def _attn_sample(q, k_cache, v_cache, k_new, v_new, g):
    bsz, t, _ = q.shape
    past = k_cache.shape[1]
    new = pl.BlockSpec((None, t, WIDTH), lambda b: (b, 0, 0))
    old = pl.BlockSpec((None, past, WIDTH), lambda b: (b, 0, 0))
    kern = functools.partial(_attn_sample_kernel, t=t, past=past)
    return pl.pallas_call(
        kern,
        grid=(bsz,),
        in_specs=[new, old, old, new, new, _const_spec((1, WIDTH))],
        out_specs=new,
        out_shape=jax.ShapeDtypeStruct((bsz, t, WIDTH), BF16),
        compiler_params=_params(("parallel",)),
        name="attn_sample",
    )(q, k_cache, v_cache, k_new, v_new, g.reshape(1, WIDTH))


def _outproj_kernel(yl_ref, oa_ref, x_ref, wt_ref, wb_ref, g_ref, wq_ref, h_ref, xn_ref, qp_ref):
    h = x_ref[...] + (_dot(yl_ref[...], wt_ref[...]) + _dot(oa_ref[...], wb_ref[...]))
    h_ref[...] = h
    xn = _rms(h, g_ref[...])
    xn_ref[...] = xn
    qp_ref[...] = _dot(xn.astype(BF16), wq_ref[...])


def _outproj(yl, oa, x, w_out_bf, g_ffn, wq_bf):
    n = x.shape[0]
    tm = min(TOKEN_TILE, n)
    nq = wq_bf.shape[1]
    row = lambda d: pl.BlockSpec((tm, d), lambda i: (i, 0))
    return pl.pallas_call(
        _outproj_kernel,
        grid=(n // tm,),
        in_specs=[row(WIDTH), row(WIDTH), row(D_MODEL),
                  _const_spec((WIDTH, D_MODEL)), _const_spec((WIDTH, D_MODEL)),
                  _const_spec((1, D_MODEL)), _const_spec(wq_bf.shape)],
        out_specs=[row(D_MODEL), row(D_MODEL), row(nq)],
        out_shape=[jax.ShapeDtypeStruct((n, D_MODEL), F32),
                   jax.ShapeDtypeStruct((n, D_MODEL), F32),
                   jax.ShapeDtypeStruct((n, nq), F32)],
        compiler_params=_params(("parallel",)),
        name="outproj",
    )(yl, oa, x, w_out_bf[:WIDTH], w_out_bf[WIDTH:], g_ffn.reshape(1, D_MODEL), wq_bf)


def _first_max(vals, tags, extra=None):
    pay = [tags] + ([extra] if extra is not None else [])
    while len(vals) > 1:
        nv, npay = [], [[] for _ in pay]
        for a in range(0, len(vals) - 1, 2):
            take_b = vals[a + 1] > vals[a]
            nv.append(jnp.maximum(vals[a], vals[a + 1]))
            for k, p in enumerate(pay):
                npay[k].append(jnp.where(take_b, p[a + 1], p[a]))
        if len(vals) % 2:
            nv.append(vals[-1])
            for k, p in enumerate(pay):
                npay[k].append(p[-1])
        vals, pay = nv, npay
    v = vals[0]
    pay = [p[0] for p in pay]
    for d in (4, 2, 1):
        vr = pltpu.roll(v, d, axis=0)
        pr = [pltpu.roll(p, d, axis=0) for p in pay]
        better = (vr > v) | ((vr == v) & (pr[0] < pay[0]))
        v = jnp.where(better, vr, v)
        pay = [jnp.where(better, a, b) for a, b in zip(pr, pay)]
    return (v, *pay)


def _topk_kernel(qp_ref, kh_ref, kl_ref, idx_ref, g_ref, top_s, top_i, best_s, out_g, out_i, *, tn):
    sub = lax.broadcasted_iota(I32, (SUBLANES, tn), 0)
    subf = sub.astype(F32)
    nslab = PEER_KEYS // SUBLANES
    key_tags = [subf + float(SUBLANES * k) for k in range(nslab)]
    ninf = jnp.full((SUBLANES, tn), -jnp.inf, F32)
    for h in range(PEER_HEADS):
        for c in range(2):
            hc = 2 * h + c
            q_hi, q_lo = _split(qp_ref[:, hc * PEER_HALF:(hc + 1) * PEER_HALF])
            kh = kh_ref[hc]
            s = _dot_nt(kh, q_hi) + _dot_nt(kh, q_lo) + _dot_nt(kl_ref[hc], q_hi)
            slabs = [s[SUBLANES * k:SUBLANES * (k + 1), :] for k in range(nslab)]
            for r in range(PEER_TOPK):
                m, am = _first_max(slabs, key_tags)
                slabs = [jnp.where(t == am, ninf, x) for x, t in zip(slabs, key_tags)]
                top_s[c, r:r + 1, :] = m[0:1, :]
                top_i[c, r:r + 1, :] = am[0:1, :]
        s0_lo, s0_hi = top_s[0, 0:SUBLANES, :], top_s[0, SUBLANES:2 * SUBLANES, :]
        i0_lo, i0_hi = top_i[0, 0:SUBLANES, :], top_i[0, SUBLANES:2 * SUBLANES, :]
        s1_lo, s1_hi = top_s[1, 0:SUBLANES, :], top_s[1, SUBLANES:2 * SUBLANES, :]
        i1_lo, i1_hi = top_i[1, 0:SUBLANES, :], top_i[1, SUBLANES:2 * SUBLANES, :]
        bcast = lambda x, a: jnp.broadcast_to(x[a:a + 1, :], (SUBLANES, tn))
        cs = [bcast(s0_lo, 0) + s1_lo, bcast(s0_lo, 0) + s1_hi]
        ci = [bcast(i0_lo, 0) * PEER_KEYS + i1_lo, bcast(i0_lo, 0) * PEER_KEYS + i1_hi]
        pos = [subf, subf + float(SUBLANES)]
        for a in range(1, SUBLANES):
            nb = PEER_TOPK // (a + 1)
            cs.append(jnp.where(sub < nb, bcast(s0_lo, a) + s1_lo, ninf))
            ci.append(bcast(i0_lo, a) * PEER_KEYS + i1_lo)
            pos.append(subf + float(a * PEER_TOPK))
        cs.append(s0_hi + bcast(s1_lo, 0))
        ci.append(i0_hi * PEER_KEYS + bcast(i1_lo, 0))
        pos.append((subf + float(SUBLANES)) * PEER_TOPK)
        for r in range(PEER_TOPK):
            m, p, e = _first_max(cs, pos, ci)
            cs = [jnp.where(t == p, ninf, x) for x, t in zip(cs, pos)]
            row = h * PEER_TOPK + r
            out_i[row:row + 1, :] = e[0:1, :]
            best_s[r:r + 1, :] = m[0:1, :]
        bs = best_s[...]
        ex = jnp.exp(bs - bs[0:1, :])
        out_g[h * PEER_TOPK:(h + 1) * PEER_TOPK, :] = ex / jnp.sum(ex, axis=0, keepdims=True)
    g_ref[...] = out_g[...].T
    idx_ref[...] = out_i[...].T.astype(I32)


def _topk(qp, sub_keys):
    n = qp.shape[0]
    tn = min(TOPK_TILE, n)
    kh, kl = _split(sub_keys.reshape(PEER_HEADS * 2, PEER_KEYS, PEER_HALF))
    kern = functools.partial(_topk_kernel, tn=tn)
    return pl.pallas_call(
        kern,
        grid=(n // tn,),
        in_specs=[pl.BlockSpec((tn, qp.shape[1]), lambda i: (i, 0)),
                  _const_spec(kh.shape), _const_spec(kl.shape)],
        out_specs=[pl.BlockSpec((tn, PEER_SEL), lambda i: (i, 0))] * 2,
        out_shape=[jax.ShapeDtypeStruct((n, PEER_SEL), I32),
                   jax.ShapeDtypeStruct((n, PEER_SEL), F32)],
        scratch_shapes=[pltpu.VMEM((2, PEER_TOPK, tn), F32),
                        pltpu.VMEM((2, PEER_TOPK, tn), F32),
                        pltpu.VMEM((PEER_TOPK, tn), F32),
                        pltpu.VMEM((PEER_SEL, tn), F32),
                        pltpu.VMEM((PEER_SEL, tn), F32)],
        compiler_params=_params(("parallel",)),
        name="peer_topk",
    )(qp, kh, kl)


def _peer_kernel(idx_ref, nxt_ref, xn_ref, g_ref, tab_ref, f_ref, buf, sem, a_s, c_s, *, tb):
    s = pl.program_id(0)
    n = pl.num_programs(0)
    slot = lax.rem(s, 2)
    other = 1 - slot
    ncol = PEER_SEL * SUBLANES

    def row_copy(e, slot_, t, r):
        return pltpu.make_async_copy(tab_ref.at[e], buf.at[slot_, t, r], sem.at[slot_, t])

    def issue(ids, slot_, t, r0, r1):
        for r in range(r0, r1):
            row_copy(ids[t, r], slot_, t, r).start(priority=r % 2)

    def drain(slot_, t):
        for r in range(PEER_SEL):
            row_copy(0, slot_, t, r).wait()

    @pl.when(s == 0)
    def _():
        def tok(t, carry):
            issue(idx_ref, 0, t, 0, PEER_SEL)
            return carry
        lax.fori_loop(0, tb, tok, 0)

    diag = (lax.broadcasted_iota(I32, (SUBLANES, ncol), 1) & (SUBLANES - 1)) == \
        lax.broadcasted_iota(I32, (SUBLANES, ncol), 0)

    for t in range(tb):
        drain(slot, t)
        x8 = xn_ref[t].astype(BF16)
        u2 = buf[slot, t, :, 0:SUBLANES, :].reshape(ncol, LANES).astype(BF16)
        m = _dot_nt(x8, u2)
        a_s[t:t + 1, :] = jnp.sum(jnp.where(diag, m, 0.0), axis=0, keepdims=True)
        issue(nxt_ref, other, t, 0, PEER_SEL // 2)

    lane = lax.broadcasted_iota(I32, (tb, LANES), 1)
    cols = []
    for c in range(ncol // LANES):
        x = a_s[:, c * LANES:(c + 1) * LANES]
        for k in (1, 2, 4):
            x = x + jnp.where((lane & k) != 0, pltpu.roll(x, k, axis=1),
                              pltpu.roll(x, LANES - k, axis=1))
        cols.append(x)
    act = jnp.concatenate(cols, axis=1)
    expand = (lax.broadcasted_iota(I32, (PEER_SEL, ncol), 1) >> 3) == \
        lax.broadcasted_iota(I32, (PEER_SEL, ncol), 0)
    expand = jnp.where(expand, 1.0, 0.0).astype(BF16)
    g_hi, g_lo = _split(g_ref[...])
    g_rep = _dot(g_hi, expand) + _dot(g_lo, expand)
    c_s[...] = g_rep * jax.nn.gelu(act)

    for t in range(tb):
        c8 = jnp.where(diag, jnp.broadcast_to(c_s[t:t + 1, :], (SUBLANES, ncol)), 0.0).astype(BF16)
        v2 = buf[slot, t, :, SUBLANES:2 * SUBLANES, :].reshape(ncol, LANES).astype(BF16)
        f_ref[t] = _dot(c8, v2)
        issue(nxt_ref, other, t, PEER_SEL // 2, PEER_SEL)

    @pl.when(s == n - 1)
    def _():
        def tok(t, carry):
            drain(other, t)
            return carry
        lax.fori_loop(0, tb, tok, 0)


def _pack_kernel(u_ref, v_ref, o_ref):
    for s in range(SUBLANES):
        o_ref[:, s, :] = u_ref[:, s * LANES:(s + 1) * LANES]
        o_ref[:, SUBLANES + s, :] = v_ref[:, s * LANES:(s + 1) * LANES]


def _pack_table(u, v):
    e = u.shape[0]
    te = min(PACK_TILE, e)
    row = pl.BlockSpec((te, D_MODEL), lambda i: (i, 0))
    return pl.pallas_call(
        _pack_kernel,
        grid=(e // te,),
        in_specs=[row, row],
        out_specs=pl.BlockSpec((te, 2 * SUBLANES, LANES), lambda i: (i, 0, 0)),
        out_shape=jax.ShapeDtypeStruct((e, 2 * SUBLANES, LANES), F32),
        compiler_params=_params(("parallel",)),
        name="peer_pack",
    )(u, v)


def _peer(idx, g, xn, table):
    n = xn.shape[0]
    tb = PEER_TILE
    steps = n // tb
    kern = functools.partial(_peer_kernel, tb=tb)
    smem = lambda fn: pl.BlockSpec((tb, PEER_SEL), fn, memory_space=pltpu.SMEM)
    slab = pl.BlockSpec((tb, SUBLANES, LANES), lambda s: (s, 0, 0))
    out = pl.pallas_call(
        kern,
        grid=(steps,),
        in_specs=[smem(lambda s: (s, 0)),
                  smem(lambda s: (jnp.minimum(s + 1, steps - 1), 0)),
                  slab,
                  pl.BlockSpec((tb, PEER_SEL), lambda s: (s, 0)),
                  pl.BlockSpec(memory_space=pl.ANY)],
        out_specs=slab,
        out_shape=jax.ShapeDtypeStruct((n, SUBLANES, LANES), F32),
        scratch_shapes=[pltpu.VMEM((2, tb, PEER_SEL, 2 * SUBLANES, LANES), F32),
                        pltpu.SemaphoreType.DMA((2, tb)),
                        pltpu.VMEM((tb, PEER_SEL * SUBLANES), F32),
                        pltpu.VMEM((tb, PEER_SEL * SUBLANES), F32)],
        compiler_params=_params(("arbitrary",)),
        name="peer_gather",
    )(idx, idx, xn.reshape(n, SUBLANES, LANES), g, table)
    return out.reshape(n, D_MODEL)


def _final_kernel(h_ref, f_ref, p_ref, gp_ref, wg_ref, wp_ref, gf_ref, y_ref):
    h = h_ref[...] + f_ref[...]
    gate = jax.nn.sigmoid(_dot(_rms(h, gp_ref[...]).astype(BF16), wg_ref[...]))
    h = h + gate * _dot(p_ref[...].astype(BF16), wp_ref[...])
    y_ref[...] = _rms(h, gf_ref[...])


def _final(h, f, p, g_ple, wg_bf, wp_bf, g_final):
    n = h.shape[0]
    tm = min(TOKEN_TILE, n)
    row = lambda d: pl.BlockSpec((tm, d), lambda i: (i, 0))
    return pl.pallas_call(
        _final_kernel,
        grid=(n // tm,),
        in_specs=[row(D_MODEL), row(D_MODEL), row(p.shape[1]), _const_spec((1, D_MODEL)),
                  _const_spec(wg_bf.shape), _const_spec(wp_bf.shape), _const_spec((1, D_MODEL))],
        out_specs=row(D_MODEL),
        out_shape=jax.ShapeDtypeStruct((n, D_MODEL), F32),
        compiler_params=_params(("parallel",)),
        name="ple_final",
    )(h, f, p, g_ple.reshape(1, D_MODEL), wg_bf, wp_bf, g_final.reshape(1, D_MODEL))


def _trunk(x, p, conv_state, lru_state, win_k, win_v, w):
    bsz, t, _ = x.shape
    n = bsz * t
    xb, gate, q, k, v, kb, vb = _inproj(x.reshape(n, D_MODEL), w["g_mix"], w["w_in"])
    seq = lambda a: a.reshape(bsz, t, WIDTH)
    xb3 = seq(xb)
    if conv_state is None:
        conv_state = jnp.zeros((bsz, 3, WIDTH), F32)
        lru_state = jnp.zeros((bsz, WIDTH), F32)
    yl, h_last = _lru(xb3, seq(gate), conv_state, lru_state, w["conv_w"], w["conv_b"],
                      w["w_ra"], w["b_ra"], w["w_ri"], w["b_ri"], w["lam"], w["g_out_lru"])
    if win_k is None:
        oa = _attn_prompt(seq(q), seq(kb), seq(vb), w["g_out_attn"])
        keep = min(MAX_WINDOW, t)
        new_k, new_v = seq(k)[:, t - keep:], seq(v)[:, t - keep:]
    else:
        past = win_k.shape[1]
        oa = _attn_sample(seq(q), win_k.reshape(bsz, past, WIDTH), win_v.reshape(bsz, past, WIDTH),
                          seq(kb), seq(vb), w["g_out_attn"])
        new_k, new_v = seq(k), seq(v)
    h1, xn2, qp = _outproj(yl.reshape(n, WIDTH), oa.reshape(n, WIDTH), x.reshape(n, D_MODEL),
                           w["w_out"], w["g_ffn"], w["w_peer_q"])
    idx, g = _topk(qp, w["sub_keys"])
    f = _peer(idx, g, xn2, w["table"])
    y = _final(h1, f, p.reshape(n, p.shape[-1]), w["g_ple"], w["w_ple_gate"], w["w_ple_proj"],
               w["g_final"])
    heads = lambda a: a.reshape(1, bsz, a.shape[1], ATTN_HEADS, HEAD_DIM)
    return (y.reshape(bsz, t, D_MODEL), xb3[:, t - 3:][None], h_last.reshape(1, bsz, WIDTH),
            heads(new_k), heads(new_v))


def kernel(x_prompt, x_sample, state_conv, state_lru, cache_win_k, cache_win_v, p_prompt, p_sample,
           g_mix, w_in, conv_w, conv_b, w_ra, b_ra, w_ri, b_ri, lru_lambda, g_out_lru, g_out_attn,
           w_out, g_ffn, w_peer_q, peer_sub_keys, peer_u, peer_v, g_ple, w_ple_gate, w_ple_proj,
           g_final):
    w = {
        "g_mix": g_mix[0].reshape(1, D_MODEL), "w_in": w_in[0].astype(BF16),
        "conv_w": conv_w[0], "conv_b": conv_b[0], "w_ra": w_ra[0], "b_ra": b_ra[0],
        "w_ri": w_ri[0], "b_ri": b_ri[0], "lam": lru_lambda[0],
        "g_out_lru": g_out_lru[0], "g_out_attn": g_out_attn[0],
        "w_out": w_out[0].astype(BF16), "g_ffn": g_ffn[0], "w_peer_q": w_peer_q[0].astype(BF16),
        "sub_keys": peer_sub_keys[0],
        "table": _pack_table(peer_u[0], peer_v[0]),
        "g_ple": g_ple[0], "w_ple_gate": w_ple_gate[0].astype(BF16),
        "w_ple_proj": w_ple_proj[0].astype(BF16), "g_final": g_final,
    }
    yp, pc, plru, pk, pv = _trunk(x_prompt, p_prompt[0], None, None, None, None, w)
    ys, sc, slru, sk, sv = _trunk(x_sample, p_sample[0], state_conv[0], state_lru[0],
                                  cache_win_k[0], cache_win_v[0], w)
    return (yp, ys, pc, plru, pk, pv, sc, slru, sk, sv)
```

```python
import functools

import jax
import jax.numpy as jnp
from jax import lax
from jax.experimental import pallas as pl
from jax.experimental.pallas import tpu as pltpu

F32 = jnp.float32
BF16 = jnp.bfloat16
I32 = jnp.int32

EPS = 1e-6
D_MODEL = 1024
HEAD_DIM = 64
ATTN_HEADS = 8
WIDTH = 512
LRU_C = 8.0
DILATED_GROUPS = ((128, 1), (512, 4), (2048, 16))
MAX_WINDOW = 2048
PEER_HEADS = 8
PEER_KEYS = 128
PEER_HALF = 128
PEER_TOPK = 16
PEER_SEL = PEER_HEADS * PEER_TOPK
NEG = -1e30

LANES = 128
SUBLANES = 8
VMEM_LIMIT = 48 * 1024 * 1024

TOKEN_TILE = 512
LRU_TILE = 256
ATTN_TILE = 256
TOPK_TILE = 128
PEER_TILE = 16
PACK_TILE = 256


def _params(sem):
    return pltpu.CompilerParams(dimension_semantics=sem, vmem_limit_bytes=VMEM_LIMIT)


def _rms(x, g):
    return x * lax.rsqrt(jnp.mean(x * x, axis=-1, keepdims=True) + EPS) * g


def _dot(a, b):
    return jnp.dot(a, b, preferred_element_type=F32)


def _dot_nt(a, b):
    return lax.dot_general(a, b, (((1,), (1,)), ((), ())), preferred_element_type=F32)


def _split(a):
    hi = a.astype(BF16)
    lo = (a - hi.astype(F32)).astype(BF16)
    return hi, lo


def _const_spec(shape):
    nd = len(shape)
    return pl.BlockSpec(shape, lambda *_: (0,) * nd)


def _inproj_kernel(x_ref, g_ref, w_ref, xb_ref, gate_ref, q_ref, k_ref, v_ref, kb_ref, vb_ref):
    xn = _rms(x_ref[...], g_ref[...]).astype(BF16)

    def proj(c):
        return _dot(xn, w_ref[:, c * WIDTH:(c + 1) * WIDTH])

    xb_ref[...] = proj(0)
    gate_ref[...] = proj(1)
    q_ref[...] = (proj(2) * (HEAD_DIM ** -0.5)).astype(BF16)
    k = proj(3)
    k_ref[...] = k
    kb_ref[...] = k.astype(BF16)
    v = proj(4)
    v_ref[...] = v
    vb_ref[...] = v.astype(BF16)


def _inproj(x, g, w_bf):
    n = x.shape[0]
    tm = min(TOKEN_TILE, n)
    row = lambda d: pl.BlockSpec((tm, d), lambda i: (i, 0))
    f = jax.ShapeDtypeStruct((n, WIDTH), F32)
    b = jax.ShapeDtypeStruct((n, WIDTH), BF16)
    return pl.pallas_call(
        _inproj_kernel,
        grid=(n // tm,),
        in_specs=[row(D_MODEL), _const_spec((1, D_MODEL)), _const_spec(w_bf.shape)],
        out_specs=[row(WIDTH)] * 7,
        out_shape=[f, f, b, f, f, b, b],
        compiler_params=_params(("parallel",)),
        name="inproj",
    )(x, g, w_bf)


def _lru_kernel(xb_ref, gate_ref, cs_ref, h0_ref, cw_ref, cb_ref, wah_ref, wal_ref, ba_ref,
                wih_ref, wil_ref, bi_ref, lam_ref, g_ref, yl_ref, hl_ref,
                xpad, hc, a_s, b_s, h_s, *, tb):
    j = pl.program_id(1)

    @pl.when(j == 0)
    def _():
        xpad[0:SUBLANES, :] = jnp.zeros((SUBLANES, WIDTH), F32)
        xpad[SUBLANES - 3:SUBLANES, :] = cs_ref[...]
        hc[...] = jnp.broadcast_to(h0_ref[...], (SUBLANES, WIDTH))

    x = xb_ref[...]
    xpad[SUBLANES:SUBLANES + tb, :] = x
    w = cw_ref[...]
    xc = cb_ref[...] + xpad[SUBLANES - 3:SUBLANES - 3 + tb, :] * w[0:1]
    xc = xc + xpad[SUBLANES - 2:SUBLANES - 2 + tb, :] * w[1:2]
    xc = xc + xpad[SUBLANES - 1:SUBLANES - 1 + tb, :] * w[2:3]
    xc = xc + x * w[3:4]
    xpad[0:SUBLANES, :] = xpad[tb:tb + SUBLANES, :]

    x_hi, x_lo = _split(xc)

    def gate_dot(wh_ref, wl_ref):
        wh = wh_ref[...]
        return _dot(x_hi, wh) + _dot(x_lo, wh) + _dot(x_hi, wl_ref[...])

    r = jax.nn.sigmoid(gate_dot(wah_ref, wal_ref) + ba_ref[...])
    i = jax.nn.sigmoid(gate_dot(wih_ref, wil_ref) + bi_ref[...])
    nl = -lam_ref[...]
    softplus = jnp.maximum(nl, 0.0) + jnp.log1p(jnp.exp(-jnp.abs(nl)))
    log_a = -LRU_C * r * softplus
    a = jnp.exp(log_a)
    a_s[...] = a
    b_s[...] = jnp.sqrt(jnp.tanh(-log_a) * (a * a + 1.0)) * (i * xc)

    rows = lax.broadcasted_iota(I32, (SUBLANES, WIDTH), 0)

    def group(gi, hb):
        off = pl.multiple_of(gi * SUBLANES, SUBLANES)
        av = a_s[pl.ds(off, SUBLANES), :]
        bv = b_s[pl.ds(off, SUBLANES), :]
        for d in (1, 2, 4):
            a_sh = pltpu.roll(av, d, axis=0)
            b_sh = pltpu.roll(bv, d, axis=0)
            m = rows >= d
            bv = jnp.where(m, av * b_sh + bv, bv)
            av = jnp.where(m, av * a_sh, av)
        h = av * hb + bv
        h_s[pl.ds(off, SUBLANES), :] = h
        return jnp.broadcast_to(h[SUBLANES - 1:SUBLANES, :], (SUBLANES, WIDTH))

    hb = lax.fori_loop(0, tb // SUBLANES, group, hc[...])
    hc[...] = hb

    y = h_s[...] * jax.nn.gelu(gate_ref[...])
    yl_ref[...] = _rms(y, g_ref[...]).astype(BF16)

    @pl.when(j == pl.num_programs(1) - 1)
    def _():
        hl_ref[...] = hb[0:1, :]


def _lru(xb, gate, conv_state, h0, conv_w, conv_b, w_ra, b_ra, w_ri, b_ri, lam, g_out):
    bsz, t, _ = xb.shape
    tb = min(LRU_TILE, t)
    wah, wal = _split(jax.scipy.linalg.block_diag(*w_ra))
    wih, wil = _split(jax.scipy.linalg.block_diag(*w_ri))
    seq = pl.BlockSpec((None, tb, WIDTH), lambda b, j: (b, j, 0))
    vec = _const_spec((1, WIDTH))
    sq = _const_spec((WIDTH, WIDTH))
    kern = functools.partial(_lru_kernel, tb=tb)
    return pl.pallas_call(
        kern,
        grid=(bsz, t // tb),
        in_specs=[seq, seq,
                  pl.BlockSpec((None, 3, WIDTH), lambda b, j: (b, 0, 0)),
                  pl.BlockSpec((None, 1, WIDTH), lambda b, j: (b, 0, 0)),
                  _const_spec((4, WIDTH)), vec, sq, sq, vec, sq, sq, vec, vec, vec],
        out_specs=[seq, pl.BlockSpec((None, 1, WIDTH), lambda b, j: (b, 0, 0))],
        out_shape=[jax.ShapeDtypeStruct((bsz, t, WIDTH), BF16),
                   jax.ShapeDtypeStruct((bsz, 1, WIDTH), F32)],
        scratch_shapes=[pltpu.VMEM((tb + SUBLANES, WIDTH), F32),
                        pltpu.VMEM((SUBLANES, WIDTH), F32),
                        pltpu.VMEM((tb, WIDTH), F32),
                        pltpu.VMEM((tb, WIDTH), F32),
                        pltpu.VMEM((tb, WIDTH), F32)],
        compiler_params=_params(("parallel", "arbitrary")),
        name="rg_lru",
    )(xb, gate, conv_state, h0.reshape(bsz, 1, WIDTH), conv_w, conv_b.reshape(1, WIDTH),
      wah, wal, b_ra.reshape(1, WIDTH), wih, wil, b_ri.reshape(1, WIDTH),
      lam.reshape(1, WIDTH), g_out.reshape(1, WIDTH))


LOCAL_WINDOW = 512
FAR_WINDOW, FAR_DIL = DILATED_GROUPS[-1]
assert all(w <= LOCAL_WINDOW for w, _ in DILATED_GROUPS[:-1]) and LOCAL_WINDOW % FAR_DIL == 0


def _tap_count(d, limit=MAX_WINDOW):
    one = jnp.ones(d.shape, F32)
    zero = jnp.zeros(d.shape, F32)
    cnt = zero
    for window, dil in DILATED_GROUPS:
        hit = jnp.where(d <= min(window, limit), one, zero)
        if dil > 1:
            hit = jnp.where((d & (dil - 1)) == 0, hit, zero)
        cnt = cnt + hit
    return jnp.where(d >= 0, cnt, zero)


def _pair_masks(rows):
    lane = lax.broadcasted_iota(I32, (rows, LANES), 1)
    return lane < HEAD_DIM


def _attn_block(q_ref, k_ref, v_ref, cnt, m_s, l_s, acc_s, rows):
    lo = _pair_masks(rows)
    valid = cnt > 0.0
    zb = jnp.zeros((rows, LANES), BF16)
    for hp in range(ATTN_HEADS // 2):
        sl = slice(hp * LANES, (hp + 1) * LANES)
        qp = q_ref[:, sl]
        kp = k_ref[:, sl]
        vp = v_ref[:, sl]
        qs = (jnp.where(lo, qp, zb), jnp.where(lo, zb, qp))
        ps, alphas = [], []
        for hh in range(2):
            h = 2 * hp + hh
            s = jnp.where(valid, _dot_nt(qs[hh], kp), NEG)
            m_old = m_s[h][:, 0:1]
            m_new = jnp.maximum(m_old, jnp.max(s, axis=-1, keepdims=True))
            p = jnp.exp(s - m_new) * cnt
            alpha = jnp.exp(m_old - m_new)
            l_new = alpha * l_s[h][:, 0:1] + jnp.sum(p, axis=-1, keepdims=True)
            m_s[h] = jnp.broadcast_to(m_new, (rows, LANES))
            l_s[h] = jnp.broadcast_to(l_new, (rows, LANES))
            ps.append(p.astype(BF16))
            alphas.append(alpha)
        pcat = jnp.concatenate(ps, axis=1)
        vbd = jnp.concatenate([jnp.where(lo, vp, zb), jnp.where(lo, zb, vp)], axis=0)
        alpha_pair = jnp.where(lo, alphas[0], alphas[1])
        acc_s[:, sl] = alpha_pair * acc_s[:, sl] + _dot(pcat, vbd)


def _attn_init(m_s, l_s, acc_s):
    m_s[...] = jnp.full(m_s.shape, NEG, F32)
    l_s[...] = jnp.zeros(l_s.shape, F32)
    acc_s[...] = jnp.zeros(acc_s.shape, F32)


def _per_head(ref, lo):
    return jnp.concatenate([jnp.where(lo, ref[2 * hp][:, 0:1], ref[2 * hp + 1][:, 0:1])
                            for hp in range(ATTN_HEADS // 2)], axis=1)


def _attn_local_kernel(q_ref, k_ref, v_ref, acc_ref, m_ref, l_ref, m_s, l_s, acc_s, *, tq, nkb):
    i = pl.program_id(1)
    j = pl.program_id(2)

    @pl.when(j == 0)
    def _():
        _attn_init(m_s, l_s, acc_s)

    kb = i - (nkb - 1) + j

    @pl.when(kb >= 0)
    def _():
        r = lax.broadcasted_iota(I32, (tq, tq), 0)
        c = lax.broadcasted_iota(I32, (tq, tq), 1)
        cnt = _tap_count((i - kb) * tq + r - c, LOCAL_WINDOW)
        _attn_block(q_ref, k_ref, v_ref, cnt, m_s, l_s, acc_s, tq)

    @pl.when(j == nkb - 1)
    def _():
        lo = _pair_masks(tq)
        acc_ref[...] = acc_s[...]
        m_ref[...] = _per_head(m_s, lo)
        l_ref[...] = _per_head(l_s, lo)


def _attn_far_kernel(q_ref, k_ref, v_ref, acc_ref, m_ref, l_ref, g_ref, o_ref, m_s, l_s, acc_s,
                     *, td):
    i = pl.program_id(2)
    j = pl.program_id(3)

    @pl.when(j == 0)
    def _():
        _attn_init(m_s, l_s, acc_s)

    kb = i - 1 + j

    @pl.when(kb >= 0)
    def _():
        r = lax.broadcasted_iota(I32, (td, td), 0)
        c = lax.broadcasted_iota(I32, (td, td), 1)
        d = (i - kb) * td + r - c
        far = jnp.where(d > LOCAL_WINDOW // FAR_DIL, 1.0, 0.0)
        cnt = jnp.where(d <= FAR_WINDOW // FAR_DIL, far, 0.0)
        _attn_block(q_ref, k_ref, v_ref, cnt, m_s, l_s, acc_s, td)

    @pl.when(j == 1)
    def _():
        lo = _pair_masks(td)
        m_far = _per_head(m_s, lo)
        m_loc = m_ref[...]
        m_tot = jnp.maximum(m_loc, m_far)
        w_loc = jnp.exp(m_loc - m_tot)
        w_far = jnp.exp(m_far - m_tot)
        num = acc_ref[...] * w_loc + acc_s[...] * w_far
        den = l_ref[...] * w_loc + _per_head(l_s, lo) * w_far
        o_ref[...] = _rms(num / den, g_ref[...]).astype(BF16)


def _attn_prompt(q, k, v, g):
    bsz, t, _ = q.shape
    tq = min(ATTN_TILE, t)
    nkb = LOCAL_WINDOW // tq + 1
    qspec = pl.BlockSpec((None, tq, WIDTH), lambda b, i, j: (b, i, 0))
    kspec = pl.BlockSpec((None, tq, WIDTH), lambda b, i, j: (b, jnp.maximum(i - (nkb - 1) + j, 0), 0))
    stat = jax.ShapeDtypeStruct((bsz, t, WIDTH), F32)
    acc, m, l = pl.pallas_call(
        functools.partial(_attn_local_kernel, tq=tq, nkb=nkb),
        grid=(bsz, t // tq, nkb),
        in_specs=[qspec, kspec, kspec],
        out_specs=[qspec] * 3,
        out_shape=[stat] * 3,
        scratch_shapes=[pltpu.VMEM((ATTN_HEADS, tq, LANES), F32),
                        pltpu.VMEM((ATTN_HEADS, tq, LANES), F32),
                        pltpu.VMEM((tq, WIDTH), F32)],
        compiler_params=_params(("parallel", "parallel", "arbitrary")),
        name="attn_local",
    )(q, k, v)

    tr = t // FAR_DIL
    td = min(ATTN_TILE, tr)
    assert td >= min(FAR_WINDOW // FAR_DIL, tr)
    view = lambda a: a.reshape(bsz, tr, FAR_DIL * WIDTH)
    dq = pl.BlockSpec((None, td, WIDTH), lambda b, r, i, j: (b, i, r))
    dk = pl.BlockSpec((None, td, WIDTH), lambda b, r, i, j: (b, jnp.maximum(i - 1 + j, 0), r))
    out = pl.pallas_call(
        functools.partial(_attn_far_kernel, td=td),
        grid=(bsz, FAR_DIL, tr // td, 2),
        in_specs=[dq, dk, dk, dq, dq, dq, _const_spec((1, WIDTH))],
        out_specs=dq,
        out_shape=jax.ShapeDtypeStruct((bsz, tr, FAR_DIL * WIDTH), BF16),
        scratch_shapes=[pltpu.VMEM((ATTN_HEADS, td, LANES), F32),
                        pltpu.VMEM((ATTN_HEADS, td, LANES), F32),
                        pltpu.VMEM((td, WIDTH), F32)],
        compiler_params=_params(("parallel", "parallel", "parallel", "arbitrary")),
        name="attn_far",
    )(view(q), view(k), view(v), view(acc), view(m), view(l), g.reshape(1, WIDTH))
    return out.reshape(bsz, t, WIDTH)


def _attn_sample_kernel(q_ref, kc_ref, vc_ref, kn_ref, vn_ref, g_ref, o_ref, *, t, past):
    lo_q = _pair_masks(t)
    lo_c = _pair_masks(past)
    rq = lax.broadcasted_iota(I32, (t, past), 0)
    cp = lax.broadcasted_iota(I32, (t, past), 1)
    cnt_c = _tap_count(past + rq - cp)
    r2 = lax.broadcasted_iota(I32, (t, t), 0)
    c2 = lax.broadcasted_iota(I32, (t, t), 1)
    cnt_n = _tap_count(r2 - c2)
    val_c = cnt_c > 0.0
    val_n = cnt_n > 0.0
    zq = jnp.zeros((t, LANES), BF16)
    zc = jnp.zeros((past, LANES), BF16)
    parts = []
    for hp in range(ATTN_HEADS // 2):
        sl = slice(hp * LANES, (hp + 1) * LANES)
        qp = q_ref[:, sl]
        kc = kc_ref[:, sl].astype(BF16)
        vc = vc_ref[:, sl].astype(BF16)
        kn = kn_ref[:, sl]
        vn = vn_ref[:, sl]
        qs = (jnp.where(lo_q, qp, zq), jnp.where(lo_q, zq, qp))
        pcs, pns, ls = [], [], []
        for hh in range(2):
            s_c = jnp.where(val_c, _dot_nt(qs[hh], kc), NEG)
            s_n = jnp.where(val_n, _dot_nt(qs[hh], kn), NEG)
            m = jnp.maximum(jnp.max(s_c, axis=-1, keepdims=True), jnp.max(s_n, axis=-1, keepdims=True))
            p_c = jnp.exp(s_c - m) * cnt_c
            p_n = jnp.exp(s_n - m) * cnt_n
            ls.append(jnp.sum(p_c, axis=-1, keepdims=True) + jnp.sum(p_n, axis=-1, keepdims=True))
            pcs.append(p_c.astype(BF16))
            pns.append(p_n.astype(BF16))
        vbd_c = jnp.concatenate([jnp.where(lo_c, vc, zc), jnp.where(lo_c, zc, vc)], axis=0)
        vbd_n = jnp.concatenate([jnp.where(lo_q, vn, zq), jnp.where(lo_q, zq, vn)], axis=0)
        pv = _dot(jnp.concatenate(pcs, axis=1), vbd_c) + _dot(jnp.concatenate(pns, axis=1), vbd_n)
        parts.append(pv / jnp.where(lo_q, ls[0], ls[1]))
    o = jnp.concatenate(parts, axis=1)
    o_ref[...] = _rms(o, g_ref[...]).astype(BF16)


def _attn_sample(q, k_cache, v_cache, k_new, v_new, g):
    bsz, t, _ = q.shape
    past = k_cache.shape[1]
    new = pl.BlockSpec((None, t, WIDTH), lambda b: (b, 0, 0))
    old = pl.BlockSpec((None, past, WIDTH), lambda b: (b, 0, 0))
    kern = functools.partial(_attn_sample_kernel, t=t, past=past)
    return pl.pallas_call(
        kern,
        grid=(bsz,),
        in_specs=[new, old, old, new, new, _const_spec((1, WIDTH))],
        out_specs=new,
        out_shape=jax.ShapeDtypeStruct((bsz, t, WIDTH), BF16),
        compiler_params=_params(("parallel",)),
        name="attn_sample",
    )(q, k_cache, v_cache, k_new, v_new, g.reshape(1, WIDTH))


def _outproj_kernel(yl_ref, oa_ref, x_ref, wt_ref, wb_ref, g_ref, wq_ref, h_ref, xn_ref, qp_ref):
    h = x_ref[...] + (_dot(yl_ref[...], wt_ref[...]) + _dot(oa_ref[...], wb_ref[...]))
    h_ref[...] = h
    xn = _rms(h, g_ref[...])
    for s in range(SUBLANES):
        xn_ref[:, s, :] = xn[:, s * LANES:(s + 1) * LANES]
    qp_ref[...] = _dot(xn.astype(BF16), wq_ref[...])


def _outproj(yl, oa, x, w_out_bf, g_ffn, wq_bf):
    n = x.shape[0]
    tm = min(TOKEN_TILE, n)
    nq = wq_bf.shape[1]
    row = lambda d: pl.BlockSpec((tm, d), lambda i: (i, 0))
    return pl.pallas_call(
        _outproj_kernel,
        grid=(n // tm,),
        in_specs=[row(WIDTH), row(WIDTH), row(D_MODEL),
                  _const_spec((WIDTH, D_MODEL)), _const_spec((WIDTH, D_MODEL)),
                  _const_spec((1, D_MODEL)), _const_spec(wq_bf.shape)],
        out_specs=[row(D_MODEL), pl.BlockSpec((tm, SUBLANES, LANES), lambda i: (i, 0, 0)), row(nq)],
        out_shape=[jax.ShapeDtypeStruct((n, D_MODEL), F32),
                   jax.ShapeDtypeStruct((n, SUBLANES, LANES), F32),
                   jax.ShapeDtypeStruct((n, nq), F32)],
        compiler_params=_params(("parallel",)),
        name="outproj",
    )(yl, oa, x, w_out_bf[:WIDTH], w_out_bf[WIDTH:], g_ffn.reshape(1, D_MODEL), wq_bf)


def _first_max(vals, tags, extra=None):
    pay = [tags] + ([extra] if extra is not None else [])
    while len(vals) > 1:
        nv, npay = [], [[] for _ in pay]
        for a in range(0, len(vals) - 1, 2):
            take_b = vals[a + 1] > vals[a]
            nv.append(jnp.maximum(vals[a], vals[a + 1]))
            for k, p in enumerate(pay):
                npay[k].append(jnp.where(take_b, p[a + 1], p[a]))
        if len(vals) % 2:
            nv.append(vals[-1])
            for k, p in enumerate(pay):
                npay[k].append(p[-1])
        vals, pay = nv, npay
    v = vals[0]
    pay = [p[0] for p in pay]
    for d in (4, 2, 1):
        vr = pltpu.roll(v, d, axis=0)
        pr = [pltpu.roll(p, d, axis=0) for p in pay]
        better = (vr > v) | ((vr == v) & (pr[0] < pay[0]))
        v = jnp.where(better, vr, v)
        pay = [jnp.where(better, a, b) for a, b in zip(pr, pay)]
    return (v, *pay)


def _topk_kernel(qp_ref, kh_ref, kl_ref, idx_ref, g_ref, top_s, top_i, best_s, out_g, out_i, *, tn):
    sub = lax.broadcasted_iota(I32, (SUBLANES, tn), 0)
    subf = sub.astype(F32)
    nslab = PEER_KEYS // SUBLANES
    key_tags = [subf + float(SUBLANES * k) for k in range(nslab)]
    ninf = jnp.full((SUBLANES, tn), -jnp.inf, F32)
    for h in range(PEER_HEADS):
        for c in range(2):
            hc = 2 * h + c
            q_hi, q_lo = _split(qp_ref[:, hc * PEER_HALF:(hc + 1) * PEER_HALF])
            kh = kh_ref[hc]
            s = _dot_nt(kh, q_hi) + _dot_nt(kh, q_lo) + _dot_nt(kl_ref[hc], q_hi)
            slabs = [s[SUBLANES * k:SUBLANES * (k + 1), :] for k in range(nslab)]
            for r in range(PEER_TOPK):
                m, am = _first_max(slabs, key_tags)
                slabs = [jnp.where(t == am, ninf, x) for x, t in zip(slabs, key_tags)]
                top_s[c, r:r + 1, :] = m[0:1, :]
                top_i[c, r:r + 1, :] = am[0:1, :]
        s0_lo, s0_hi = top_s[0, 0:SUBLANES, :], top_s[0, SUBLANES:2 * SUBLANES, :]
        i0_lo, i0_hi = top_i[0, 0:SUBLANES, :], top_i[0, SUBLANES:2 * SUBLANES, :]
        s1_lo, s1_hi = top_s[1, 0:SUBLANES, :], top_s[1, SUBLANES:2 * SUBLANES, :]
        i1_lo, i1_hi = top_i[1, 0:SUBLANES, :], top_i[1, SUBLANES:2 * SUBLANES, :]
        bcast = lambda x, a: jnp.broadcast_to(x[a:a + 1, :], (SUBLANES, tn))
        cs = [bcast(s0_lo, 0) + s1_lo, bcast(s0_lo, 0) + s1_hi]
        ci = [bcast(i0_lo, 0) * PEER_KEYS + i1_lo, bcast(i0_lo, 0) * PEER_KEYS + i1_hi]
        pos = [subf, subf + float(SUBLANES)]
        for a in range(1, SUBLANES):
            nb = PEER_TOPK // (a + 1)
            cs.append(jnp.where(sub < nb, bcast(s0_lo, a) + s1_lo, ninf))
            ci.append(bcast(i0_lo, a) * PEER_KEYS + i1_lo)
            pos.append(subf + float(a * PEER_TOPK))
        cs.append(s0_hi + bcast(s1_lo, 0))
        ci.append(i0_hi * PEER_KEYS + bcast(i1_lo, 0))
        pos.append((subf + float(SUBLANES)) * PEER_TOPK)
        for r in range(PEER_TOPK):
            m, p, e = _first_max(cs, pos, ci)
            cs = [jnp.where(t == p, ninf, x) for x, t in zip(cs, pos)]
            row = h * PEER_TOPK + r
            out_i[row:row + 1, :] = e[0:1, :]
            best_s[r:r + 1, :] = m[0:1, :]
        bs = best_s[...]
        ex = jnp.exp(bs - bs[0:1, :])
        out_g[h * PEER_TOPK:(h + 1) * PEER_TOPK, :] = ex / jnp.sum(ex, axis=0, keepdims=True)
    g_ref[...] = out_g[...].T
    idx_ref[...] = out_i[...].T.astype(I32)


def _topk(qp, sub_keys):
    n = qp.shape[0]
    tn = min(TOPK_TILE, n)
    kh, kl = _split(sub_keys.reshape(PEER_HEADS * 2, PEER_KEYS, PEER_HALF))
    kern = functools.partial(_topk_kernel, tn=tn)
    return pl.pallas_call(
        kern,
        grid=(n // tn,),
        in_specs=[pl.BlockSpec((tn, qp.shape[1]), lambda i: (i, 0)),
                  _const_spec(kh.shape), _const_spec(kl.shape)],
        out_specs=[pl.BlockSpec((tn, PEER_SEL), lambda i: (i, 0))] * 2,
        out_shape=[jax.ShapeDtypeStruct((n, PEER_SEL), I32),
                   jax.ShapeDtypeStruct((n, PEER_SEL), F32)],
        scratch_shapes=[pltpu.VMEM((2, PEER_TOPK, tn), F32),
                        pltpu.VMEM((2, PEER_TOPK, tn), F32),
                        pltpu.VMEM((PEER_TOPK, tn), F32),
                        pltpu.VMEM((PEER_SEL, tn), F32),
                        pltpu.VMEM((PEER_SEL, tn), F32)],
        compiler_params=_params(("parallel",)),
        name="peer_topk",
    )(qp, kh, kl)


def _peer_kernel(idx_ref, nxt_ref, xn_ref, g_ref, tab_ref, f_ref, buf, sem, a_s, c_s, *, tb):
    s = pl.program_id(0)
    n = pl.num_programs(0)
    slot = lax.rem(s, 2)
    other = 1 - slot
    ncol = PEER_SEL * SUBLANES

    def row_copy(e, slot_, t, r):
        return pltpu.make_async_copy(tab_ref.at[e], buf.at[slot_, t, r], sem.at[slot_, t])

    def issue(ids, slot_, t, r0, r1):
        for r in range(r0, r1):
            row_copy(ids[t, r], slot_, t, r).start(priority=r % 2)

    def drain(slot_, t):
        for r in range(PEER_SEL):
            row_copy(0, slot_, t, r).wait()

    @pl.when(s == 0)
    def _():
        def tok(t, carry):
            issue(idx_ref, 0, t, 0, PEER_SEL)
            return carry
        lax.fori_loop(0, tb, tok, 0)

    diag = (lax.broadcasted_iota(I32, (SUBLANES, ncol), 1) & (SUBLANES - 1)) == \
        lax.broadcasted_iota(I32, (SUBLANES, ncol), 0)

    for t in range(tb):
        drain(slot, t)
        x8 = xn_ref[t].astype(BF16)
        u2 = buf[slot, t, :, 0:SUBLANES, :].reshape(ncol, LANES).astype(BF16)
        m = _dot_nt(x8, u2)
        a_s[t:t + 1, :] = jnp.sum(jnp.where(diag, m, 0.0), axis=0, keepdims=True)
        issue(nxt_ref, other, t, 0, PEER_SEL // 2)

    lane = lax.broadcasted_iota(I32, (tb, LANES), 1)
    cols = []
    for c in range(ncol // LANES):
        x = a_s[:, c * LANES:(c + 1) * LANES]
        for k in (1, 2, 4):
            x = x + jnp.where((lane & k) != 0, pltpu.roll(x, k, axis=1),
                              pltpu.roll(x, LANES - k, axis=1))
        cols.append(x)
    act = jnp.concatenate(cols, axis=1)
    expand = (lax.broadcasted_iota(I32, (PEER_SEL, ncol), 1) >> 3) == \
        lax.broadcasted_iota(I32, (PEER_SEL, ncol), 0)
    expand = jnp.where(expand, 1.0, 0.0).astype(BF16)
    g_hi, g_lo = _split(g_ref[...])
    g_rep = _dot(g_hi, expand) + _dot(g_lo, expand)
    c_s[...] = g_rep * jax.nn.gelu(act)

    for t in range(tb):
        c8 = jnp.where(diag, jnp.broadcast_to(c_s[t:t + 1, :], (SUBLANES, ncol)), 0.0).astype(BF16)
        v2 = buf[slot, t, :, SUBLANES:2 * SUBLANES, :].reshape(ncol, LANES).astype(BF16)
        f_ref[t] = _dot(c8, v2)
        issue(nxt_ref, other, t, PEER_SEL // 2, PEER_SEL)

    @pl.when(s == n - 1)
    def _():
        def tok(t, carry):
            drain(other, t)
            return carry
        lax.fori_loop(0, tb, tok, 0)


def _pack_kernel(u_ref, v_ref, o_ref):
    for s in range(SUBLANES):
        o_ref[:, s, :] = u_ref[:, s * LANES:(s + 1) * LANES]
        o_ref[:, SUBLANES + s, :] = v_ref[:, s * LANES:(s + 1) * LANES]


def _pack_table(u, v):
    e = u.shape[0]
    te = min(PACK_TILE, e)
    row = pl.BlockSpec((te, D_MODEL), lambda i: (i, 0))
    return pl.pallas_call(
        _pack_kernel,
        grid=(e // te,),
        in_specs=[row, row],
        out_specs=pl.BlockSpec((te, 2 * SUBLANES, LANES), lambda i: (i, 0, 0)),
        out_shape=jax.ShapeDtypeStruct((e, 2 * SUBLANES, LANES), F32),
        compiler_params=_params(("parallel",)),
        name="peer_pack",
    )(u, v)


def _peer(idx, g, xn, table):
    n = xn.shape[0]
    tb = PEER_TILE
    steps = n // tb
    kern = functools.partial(_peer_kernel, tb=tb)
    smem = lambda fn: pl.BlockSpec((tb, PEER_SEL), fn, memory_space=pltpu.SMEM)
    slab = pl.BlockSpec((tb, SUBLANES, LANES), lambda s: (s, 0, 0))
    out = pl.pallas_call(
        kern,
        grid=(steps,),
        in_specs=[smem(lambda s: (s, 0)),
                  smem(lambda s: (jnp.minimum(s + 1, steps - 1), 0)),
                  slab,
                  pl.BlockSpec((tb, PEER_SEL), lambda s: (s, 0)),
                  pl.BlockSpec(memory_space=pl.ANY)],
        out_specs=slab,
        out_shape=jax.ShapeDtypeStruct((n, SUBLANES, LANES), F32),
        scratch_shapes=[pltpu.VMEM((2, tb, PEER_SEL, 2 * SUBLANES, LANES), F32),
                        pltpu.SemaphoreType.DMA((2, tb)),
                        pltpu.VMEM((tb, PEER_SEL * SUBLANES), F32),
                        pltpu.VMEM((tb, PEER_SEL * SUBLANES), F32)],
        compiler_params=_params(("arbitrary",)),
        name="peer_gather",
    )(idx, idx, xn, g, table)
    return out


def _final_kernel(h_ref, f_ref, p_ref, gp_ref, wg_ref, wp_ref, gf_ref, y_ref):
    f = jnp.concatenate([f_ref[:, s, :] for s in range(SUBLANES)], axis=1)
    h = h_ref[...] + f
    gate = jax.nn.sigmoid(_dot(_rms(h, gp_ref[...]).astype(BF16), wg_ref[...]))
    h = h + gate * _dot(p_ref[...].astype(BF16), wp_ref[...])
    y_ref[...] = _rms(h, gf_ref[...])


def _final(h, f, p, g_ple, wg_bf, wp_bf, g_final):
    n = h.shape[0]
    tm = min(TOKEN_TILE, n)
    row = lambda d: pl.BlockSpec((tm, d), lambda i: (i, 0))
    return pl.pallas_call(
        _final_kernel,
        grid=(n // tm,),
        in_specs=[row(D_MODEL), pl.BlockSpec((tm, SUBLANES, LANES), lambda i: (i, 0, 0)),
                  row(p.shape[1]), _const_spec((1, D_MODEL)),
                  _const_spec(wg_bf.shape), _const_spec(wp_bf.shape), _const_spec((1, D_MODEL))],
        out_specs=row(D_MODEL),
        out_shape=jax.ShapeDtypeStruct((n, D_MODEL), F32),
        compiler_params=_params(("parallel",)),
        name="ple_final",
    )(h, f, p, g_ple.reshape(1, D_MODEL), wg_bf, wp_bf, g_final.reshape(1, D_MODEL))


def _trunk(x, p, conv_state, lru_state, win_k, win_v, w):
    bsz, t, _ = x.shape
    n = bsz * t
    xb, gate, q, k, v, kb, vb = _inproj(x.reshape(n, D_MODEL), w["g_mix"], w["w_in"])
    seq = lambda a: a.reshape(bsz, t, WIDTH)
    xb3 = seq(xb)
    if conv_state is None:
        conv_state = jnp.zeros((bsz, 3, WIDTH), F32)
        lru_state = jnp.zeros((bsz, WIDTH), F32)
    yl, h_last = _lru(xb3, seq(gate), conv_state, lru_state, w["conv_w"], w["conv_b"],
                      w["w_ra"], w["b_ra"], w["w_ri"], w["b_ri"], w["lam"], w["g_out_lru"])
    if win_k is None:
        oa = _attn_prompt(seq(q), seq(kb), seq(vb), w["g_out_attn"])
        keep = min(MAX_WINDOW, t)
        new_k, new_v = seq(k)[:, t - keep:], seq(v)[:, t - keep:]
    else:
        past = win_k.shape[1]
        oa = _attn_sample(seq(q), win_k.reshape(bsz, past, WIDTH), win_v.reshape(bsz, past, WIDTH),
                          seq(kb), seq(vb), w["g_out_attn"])
        new_k, new_v = seq(k), seq(v)
    h1, xn2, qp = _outproj(yl.reshape(n, WIDTH), oa.reshape(n, WIDTH), x.reshape(n, D_MODEL),
                           w["w_out"], w["g_ffn"], w["w_peer_q"])
    idx, g = _topk(qp, w["sub_keys"])
    f = _peer(idx, g, xn2, w["table"])
    y = _final(h1, f, p.reshape(n, p.shape[-1]), w["g_ple"], w["w_ple_gate"], w["w_ple_proj"],
               w["g_final"])
    heads = lambda a: a.reshape(1, bsz, a.shape[1], ATTN_HEADS, HEAD_DIM)
    return (y.reshape(bsz, t, D_MODEL), xb3[:, t - 3:][None], h_last.reshape(1, bsz, WIDTH),
            heads(new_k), heads(new_v))


def kernel(x_prompt, x_sample, state_conv, state_lru, cache_win_k, cache_win_v, p_prompt, p_sample,
           g_mix, w_in, conv_w, conv_b, w_ra, b_ra, w_ri, b_ri, lru_lambda, g_out_lru, g_out_attn,
           w_out, g_ffn, w_peer_q, peer_sub_keys, peer_u, peer_v, g_ple, w_ple_gate, w_ple_proj,
           g_final):
    w = {
        "g_mix": g_mix[0].reshape(1, D_MODEL), "w_in": w_in[0].astype(BF16),
        "conv_w": conv_w[0], "conv_b": conv_b[0], "w_ra": w_ra[0], "b_ra": b_ra[0],
        "w_ri": w_ri[0], "b_ri": b_ri[0], "lam": lru_lambda[0],
        "g_out_lru": g_out_lru[0], "g_out_attn": g_out_attn[0],
        "w_out": w_out[0].astype(BF16), "g_ffn": g_ffn[0], "w_peer_q": w_peer_q[0].astype(BF16),
        "sub_keys": peer_sub_keys[0],
        "table": _pack_table(peer_u[0], peer_v[0]),
        "g_ple": g_ple[0], "w_ple_gate": w_ple_gate[0].astype(BF16),
        "w_ple_proj": w_ple_proj[0].astype(BF16), "g_final": g_final,
    }
    yp, pc, plru, pk, pv = _trunk(x_prompt, p_prompt[0], None, None, None, None, w)
    ys, sc, slru, sk, sv = _trunk(x_sample, p_sample[0], state_conv[0], state_lru[0],
                                  cache_win_k[0], cache_win_v[0], w)
    return (yp, ys, pc, plru, pk, pv, sc, slru, sk, sv)
```
